```python
import math
import jax, jax.numpy as jnp
from jax import lax
import numpy as np

D_MODEL = 1024
BATCH = 4
SEQ = 8192
DEPTH = 2
DEC_BATCH = 32
DEC_SEQ = 8
PAST_LEN = 16384
PAGE_SIZE = 128

N_A_LAYERS = DEPTH // 2
N_B_LAYERS = DEPTH - N_A_LAYERS
N_DENSE_LAYERS = (DEPTH + 1) // 2
N_MOE_LAYERS = DEPTH // 2

RW_HEAD_DIM = 64
RW_HEADS = D_MODEL // RW_HEAD_DIM
DECAY_LORA = max(32, int(round(1.8 * D_MODEL ** 0.5 / 32)) * 32)
AAA_LORA = max(32, int(round(1.8 * D_MODEL ** 0.5 / 32)) * 32)
GATE_LORA = max(32, int(round(0.6 * D_MODEL ** 0.8 / 32)) * 32)
RW_GN_EPS = 64e-5

SB_HEAD_DIM = 64
SB_HEADS = D_MODEL // SB_HEAD_DIM
Q_BLOCK = 128
SB_BIAS_INIT = -8.0

D_FF = 256 * (-(-(8 * D_MODEL // 3) // 256))
N_EXPERTS = 8
TOP_K = 2
D_FF_EXPERT = 7 * D_MODEL // 2
RMS_EPS = 1e-6

kernel_name = 'yoco_rwkv7_stickbreaking_moe_step'


def rms_norm(x, g):
    xf = x.astype(jnp.float32)
    y = xf * lax.rsqrt(jnp.mean(xf * xf, axis=-1, keepdims=True) + RMS_EPS)
    return (y * g.astype(jnp.float32)).astype(x.dtype)


def swiglu(xn, w_in, w_out):
    g, u = jnp.split(xn @ w_in, 2, axis=-1)
    return (jax.nn.silu(g) * u) @ w_out


def moe_swiglu(xn, router, w_in, w_out):
    logits = (xn @ router).astype(jnp.float32)
    top_v, top_i = lax.top_k(logits, TOP_K)
    gates = jax.nn.softmax(top_v, axis=-1)
    comb = jnp.sum(jax.nn.one_hot(top_i, N_EXPERTS, dtype=jnp.float32) * gates[..., None], axis=-2)
    out = jnp.zeros(xn.shape, jnp.float32)
    for e in range(N_EXPERTS):
        out = out + comb[..., e:e + 1] * swiglu(xn, w_in[e], w_out[e]).astype(jnp.float32)
    return out.astype(xn.dtype)


def rwkv7_time_mix(xn, shift0, s0, mix, w_r, w_k, w_v, w_o, w0, w1, w2, a0, a1, a2,
                   g1, g2, k_k, k_a, r_k, ln_w, ln_b):
    B, T, D = xn.shape
    H, N = RW_HEADS, RW_HEAD_DIM
    f32 = jnp.float32
    xf = xn.astype(f32)
    prev = jnp.concatenate([shift0[:, None, :].astype(f32), xf[:, :-1]], axis=1)
    dx = prev - xf
    mix = mix.astype(f32)
    xr = xf + dx * mix[0]
    xw = xf + dx * mix[1]
    xk = xf + dx * mix[2]
    xv = xf + dx * mix[3]
    xa = xf + dx * mix[4]
    xg = xf + dx * mix[5]
    r = xr @ w_r
    k = xk @ w_k
    v = xv @ w_v
    log_w = -jax.nn.softplus(-(w0 + jnp.tanh(xw @ w1) @ w2)) - 0.5
    decay = jnp.exp(-jnp.exp(log_w))
    a = jax.nn.sigmoid(a0 + (xa @ a1) @ a2)
    g = jax.nn.sigmoid(xg @ g1) @ g2
    kk = (k * k_k).reshape(B, T, H, N)
    kk = kk / jnp.maximum(jnp.sqrt(jnp.sum(kk * kk, axis=-1, keepdims=True)), 1e-12)
    k = k * (1.0 + (a - 1.0) * k_a)
    r_h = r.reshape(B, T, H, N)
    k_h = k.reshape(B, T, H, N)
    v_h = v.reshape(B, T, H, N)
    w_h = decay.reshape(B, T, H, N)
    a_h = a.reshape(B, T, H, N)
    seq = tuple(jnp.moveaxis(t, 1, 0) for t in (r_h, w_h, k_h, v_h, -kk, kk * a_h))

    def step(S, inp):
        r_t, w_t, k_t, v_t, a_t, b_t = inp
        sa = jnp.einsum('bhvk,bhk->bhv', S, a_t)
        S = S * w_t[:, :, None, :] + sa[..., None] * b_t[:, :, None, :] + v_t[..., None] * k_t[:, :, None, :]
        return S, jnp.einsum('bhvk,bhk->bhv', S, r_t)

    s_fin, y = lax.scan(step, s0.astype(f32), seq)
    y = jnp.moveaxis(y, 0, 1)
    mu = jnp.mean(y, axis=-1, keepdims=True)
    var = jnp.mean(jnp.square(y - mu), axis=-1, keepdims=True)
    y = ((y - mu) * lax.rsqrt(var + RW_GN_EPS)).reshape(B, T, D) * ln_w + ln_b
    bonus = jnp.sum(r_h * k_h * r_k, axis=-1, keepdims=True) * v_h
    y = y + bonus.reshape(B, T, D)
    out = (y * g) @ w_o
    return out.astype(xn.dtype), s_fin.astype(s0.dtype), xn[:, -1]


def stick_breaking_attention(q, segments, q_pos, bias):
    B, Tq, H, Dh = q.shape
    f32 = jnp.float32
    qb = min(Q_BLOCK, Tq)
    nb = -(-Tq // qb)
    pad = nb * qb - Tq
    scale = 1.0 / math.sqrt(Dh)
    qf = jnp.pad(q.astype(f32), ((0, 0), (0, pad), (0, 0), (0, 0))).reshape(B, nb, qb, H, Dh).swapaxes(0, 1)
    pf = jnp.pad(q_pos, (0, pad), constant_values=-1).reshape(nb, qb)
    segs = [(k.astype(f32), v.astype(f32), kp) for (k, v, kp) in segments]
    bias_f = bias.astype(f32)[None, :, None, None]

    def block(args):
        qblk, pblk = args
        out = jnp.zeros((B, qb, H, Dh), f32)
        carry = jnp.zeros((B, H, qb, 1), f32)
        for k, v, kp in reversed(segs):
            z = jnp.einsum('bqhd,bkhd->bhqk', qblk, k) * scale + bias_f
            mask = (kp[None, :] < pblk[:, None])[None, None]
            log_beta = jax.nn.log_sigmoid(z)
            log_keep = jnp.where(mask, log_beta - z, 0.0)
            tail = lax.cumsum(log_keep, axis=3, reverse=True) - log_keep + carry
            A = jnp.where(mask, jnp.exp(log_beta + tail), 0.0)
            out = out + jnp.einsum('bhqk,bkhd->bqhd', A, v)
            carry = carry + jnp.sum(log_keep, axis=3, keepdims=True)
        return out

    o = lax.map(block, (qf, pf))
    return o.swapaxes(0, 1).reshape(B, nb * qb, H, Dh)[:, :Tq]


def decoder_step(x, shift_states, wkv_states, past_segments, pos0, p):
    B, T, D = x.shape
    q_pos = pos0 + jnp.arange(T, dtype=jnp.int32)
    new_shift, new_wkv = [], []
    k_sh = v_sh = None
    for i in range(DEPTH):
        xn = rms_norm(x, p['norm_mix'][i])
        if i < N_A_LAYERS:
            a = i
            h, s_fin, last = rwkv7_time_mix(
                xn, shift_states[a], wkv_states[a], p['rw_mix'][a], p['rw_w_r'][a], p['rw_w_k'][a],
                p['rw_w_v'][a], p['rw_w_o'][a], p['rw_w0'][a], p['rw_w1'][a], p['rw_w2'][a],
                p['rw_a0'][a], p['rw_a1'][a], p['rw_a2'][a], p['rw_g1'][a], p['rw_g2'][a],
                p['rw_k_k'][a], p['rw_k_a'][a], p['rw_r_k'][a], p['rw_ln_w'][a], p['rw_ln_b'][a])
            new_shift.append(last)
            new_wkv.append(s_fin)
        else:
            b = i - N_A_LAYERS
            q = (xn @ p['sb_w_q'][b]).reshape(B, T, SB_HEADS, SB_HEAD_DIM)
            q = rms_norm(q, p['sb_q_norm'][b])
            o = stick_breaking_attention(q, past_segments + [(k_sh, v_sh, q_pos)], q_pos, p['sb_bias'][b])
            h = o.reshape(B, T, D).astype(x.dtype) @ p['sb_w_o'][b]
        x = x + h
        xn = rms_norm(x, p['norm_ffn'][i])
        if i % 2 == 0:
            f = swiglu(xn, p['ffn_w_in'][i // 2], p['ffn_w_out'][i // 2])
        else:
            f = moe_swiglu(xn, p['moe_router'][i // 2], p['moe_w_in'][i // 2], p['moe_w_out'][i // 2])
        x = x + f
        if i == N_A_LAYERS - 1:
            kvn = rms_norm(x, p['kv_norm'])
            kv = (kvn @ p['kv_w']).reshape(B, T, 2, SB_HEADS, SB_HEAD_DIM)
            k_sh = rms_norm(kv[:, :, 0], p['kv_k_norm'])
            v_sh = kv[:, :, 1]
    return x, jnp.stack(new_shift), jnp.stack(new_wkv), k_sh, v_sh


def setup_inputs(seed: int = 0) -> dict:
    key = jax.random.key(seed)
    ks = iter(jax.random.split(key, 64))
    f32 = jnp.float32

    def nrm(shape, scale):
        return jax.random.normal(next(ks), shape, f32) * scale

    D, NA, NB, ND, NM = D_MODEL, N_A_LAYERS, N_B_LAYERS, N_DENSE_LAYERS, N_MOE_LAYERS
    n_pages = PAST_LEN // PAGE_SIZE
    n_used = DEC_BATCH * n_pages
    n_pool = n_used + max(1, n_used // 4)
    chan = jnp.arange(D, dtype=f32) / (D - 1)
    w0_base = -7.0 + 5.0 * chan ** 0.85 + 0.5
    return {
        'x_prompt': nrm((BATCH, SEQ, D), 1.0),
        'x_sample': nrm((DEC_BATCH, DEC_SEQ, D), 1.0),
        'cache_k': nrm((n_pool, PAGE_SIZE, SB_HEADS, SB_HEAD_DIM), 1.0),
        'cache_v': nrm((n_pool, PAGE_SIZE, SB_HEADS, SB_HEAD_DIM), 1.0),
        'state_wkv': nrm((NA, DEC_BATCH, RW_HEADS, RW_HEAD_DIM, RW_HEAD_DIM), 0.1),
        'state_shift': nrm((NA, DEC_BATCH, D), 1.0),
        'page_table': jax.random.permutation(next(ks), n_pool)[:n_used].reshape(DEC_BATCH, n_pages).astype(jnp.int32),
        'norm_mix': 1.0 + nrm((DEPTH, D), 0.05),
        'norm_ffn': 1.0 + nrm((DEPTH, D), 0.05),
        'rw_mix': jax.random.uniform(next(ks), (NA, 6, D), f32),
        'rw_w_r': nrm((NA, D, D), D ** -0.5),
        'rw_w_k': nrm((NA, D, D), D ** -0.5),
        'rw_w_v': nrm((NA, D, D), D ** -0.5),
        'rw_w_o': nrm((NA, D, D), D ** -0.5),
        'rw_w0': w0_base[None, :] + nrm((NA, D), 0.1),
        'rw_w1': nrm((NA, D, DECAY_LORA), D ** -0.5),
        'rw_w2': nrm((NA, DECAY_LORA, D), 0.5 * DECAY_LORA ** -0.5),
        'rw_a0': nrm((NA, D), 0.1),
        'rw_a1': nrm((NA, D, AAA_LORA), D ** -0.5),
        'rw_a2': nrm((NA, AAA_LORA, D), 0.5 * AAA_LORA ** -0.5),
        'rw_g1': nrm((NA, D, GATE_LORA), D ** -0.5),
        'rw_g2': nrm((NA, GATE_LORA, D), GATE_LORA ** -0.5),
        'rw_k_k': 0.85 + nrm((NA, D), 0.05),
        'rw_k_a': 1.0 + nrm((NA, D), 0.05),
        'rw_r_k': nrm((NA, RW_HEADS, RW_HEAD_DIM), 0.1),
        'rw_ln_w': 1.0 + nrm((NA, D), 0.05),
        'rw_ln_b': nrm((NA, D), 0.02),
        'kv_norm': 1.0 + nrm((D,), 0.05),
        'kv_w': nrm((D, 2 * D), D ** -0.5),
        'kv_k_norm': 1.0 + nrm((SB_HEAD_DIM,), 0.05),
        'sb_w_q': nrm((NB, D, D), D ** -0.5),
        'sb_q_norm': 1.0 + nrm((NB, SB_HEAD_DIM), 0.05),
        'sb_bias': SB_BIAS_INIT + nrm((NB, SB_HEADS), 0.5),
        'sb_w_o': nrm((NB, D, D), D ** -0.5),
        'ffn_w_in': nrm((ND, D, 2 * D_FF), D ** -0.5),
        'ffn_w_out': nrm((ND, D_FF, D), D_FF ** -0.5),
        'moe_router': nrm((NM, D, N_EXPERTS), D ** -0.5),
        'moe_w_in': nrm((NM, N_EXPERTS, D, 2 * D_FF_EXPERT), D ** -0.5),
        'moe_w_out': nrm((NM, N_EXPERTS, D_FF_EXPERT, D), D_FF_EXPERT ** -0.5),
    }


def reference(x_prompt, x_sample, cache_k, cache_v, state_wkv, state_shift, page_table,
              norm_mix, norm_ffn, rw_mix, rw_w_r, rw_w_k, rw_w_v, rw_w_o, rw_w0, rw_w1, rw_w2,
              rw_a0, rw_a1, rw_a2, rw_g1, rw_g2, rw_k_k, rw_k_a, rw_r_k, rw_ln_w, rw_ln_b,
              kv_norm, kv_w, kv_k_norm, sb_w_q, sb_q_norm, sb_bias, sb_w_o, ffn_w_in, ffn_w_out,
              moe_router, moe_w_in, moe_w_out):
    p = dict(norm_mix=norm_mix, norm_ffn=norm_ffn, rw_mix=rw_mix, rw_w_r=rw_w_r, rw_w_k=rw_w_k,
             rw_w_v=rw_w_v, rw_w_o=rw_w_o, rw_w0=rw_w0, rw_w1=rw_w1, rw_w2=rw_w2, rw_a0=rw_a0,
             rw_a1=rw_a1, rw_a2=rw_a2, rw_g1=rw_g1, rw_g2=rw_g2, rw_k_k=rw_k_k, rw_k_a=rw_k_a,
             rw_r_k=rw_r_k, rw_ln_w=rw_ln_w, rw_ln_b=rw_ln_b, kv_norm=kv_norm, kv_w=kv_w,
             kv_k_norm=kv_k_norm, sb_w_q=sb_w_q, sb_q_norm=sb_q_norm, sb_bias=sb_bias, sb_w_o=sb_w_o,
             ffn_w_in=ffn_w_in, ffn_w_out=ffn_w_out, moe_router=moe_router, moe_w_in=moe_w_in,
             moe_w_out=moe_w_out)
    n_prompt = x_prompt.shape[0]
    shift0 = jnp.zeros((N_A_LAYERS, n_prompt, D_MODEL), x_prompt.dtype)
    wkv0 = jnp.zeros((N_A_LAYERS, n_prompt, RW_HEADS, RW_HEAD_DIM, RW_HEAD_DIM), x_prompt.dtype)
    y_prompt, shift_p, wkv_p, k_p, v_p = decoder_step(x_prompt, shift0, wkv0, [], 0, p)
    n_seq = page_table.shape[0]
    k_past = jnp.take(cache_k, page_table, axis=0).reshape(n_seq, -1, SB_HEADS, SB_HEAD_DIM)
    v_past = jnp.take(cache_v, page_table, axis=0).reshape(n_seq, -1, SB_HEADS, SB_HEAD_DIM)
    past_len = k_past.shape[1]
    past_pos = jnp.arange(past_len, dtype=jnp.int32)
    y_sample, shift_s, wkv_s, k_s, v_s = decoder_step(
        x_sample, state_shift, state_wkv, [(k_past, v_past, past_pos)], past_len, p)
    return (y_prompt, y_sample, wkv_p, shift_p, k_p, v_p, wkv_s, shift_s, k_s, v_s)
```

```python
import functools
import math

import jax
import jax.numpy as jnp
from jax import lax
from jax.experimental import pallas as pl
from jax.experimental.pallas import tpu as pltpu

F32 = jnp.float32
BF16 = jnp.bfloat16

HEAD_DIM = 64
LANES = 128
SUBLANES = 8
RMS_EPS = 1e-6
RW_GN_EPS = 64e-5
TOP_K = 2
SCAN_CHUNK = 64
SCAN_PIECES = 2
VMEM_LIMIT = 52 * 1024 * 1024


def _cparams(sem):
    return pltpu.CompilerParams(dimension_semantics=sem, vmem_limit_bytes=VMEM_LIMIT)


def _dot(a, b):
    return jnp.dot(a, b, preferred_element_type=F32)


def _dot_nt(a, b):
    return lax.dot_general(a, b, (((1,), (1,)), ((), ())), preferred_element_type=F32)


def _dot_tn(a, b):
    return lax.dot_general(a, b, (((0,), (0,)), ((), ())), preferred_element_type=F32)


def _split(x, n):
    parts = []
    rem = x
    for i in range(n):
        p = rem.astype(BF16)
        parts.append(p)
        if i + 1 < n:
            rem = rem - p.astype(F32)
    return parts


def _mm(a, b, kind="nn", na=3, nb=3):
    f = {"nn": _dot, "nt": _dot_nt, "tn": _dot_tn}[kind]
    ap = _split(a, na) if a.dtype != BF16 else [a]
    bp = _split(b, nb) if b.dtype != BF16 else [b]
    keep = max(len(ap), len(bp))
    out = None
    for i, x in enumerate(ap):
        for j, y in enumerate(bp):
            if i + j < keep:
                t = f(x, y)
                out = t if out is None else out + t
    return out


def _rms(x, gain):
    return x * lax.rsqrt(jnp.mean(x * x, axis=-1, keepdims=True) + RMS_EPS) * gain


def _softplus(z):
    return jnp.maximum(z, 0.0) + jnp.log(1.0 + jnp.exp(-jnp.abs(z)))


def _sigmoid(z):
    return 1.0 / (1.0 + jnp.exp(-z))


def _head_sum(x, e, et):
    s = _mm(x, e, na=3)
    return _mm(s, et, na=3)


def _head_onehot(d):
    h = d // HEAD_DIM
    e = (jnp.arange(d)[:, None] // HEAD_DIM == jnp.arange(LANES)[None, :]).astype(BF16)
    del h
    return e, e.T


def _rwkv_pre_kernel(x_ref, xp_ref, sh_ref, nm_ref, mix_ref, wr_ref, wk_ref, wv_ref, w0_ref, w1_ref,
                     w2_ref, a0_ref, a1_ref, a2_ref, g1_ref, g2_ref, kk_ref, ka_ref, rk_ref, e_ref,
                     et_ref, r_o, lw_o, k_o, v_o, a_o, b_o, g_o, bonus_o, last_o):
    t = pl.program_id(1)
    gain = nm_ref[...]
    x = x_ref[0]
    tt = x.shape[0]
    xn = _rms(x, gain)
    prev_tile_last = _rms(xp_ref[0][SUBLANES - 1:SUBLANES, :], gain)
    first = jnp.where(t == 0, sh_ref[0], prev_tile_last)
    row = lax.broadcasted_iota(jnp.int32, xn.shape, 0)
    prev = jnp.where(row == 0, first, pltpu.roll(xn, 1, 0))
    dx = prev - xn

    def mixed(i):
        return (xn + dx * mix_ref[i:i + 1, :]).astype(BF16)

    r = _dot(mixed(0), wr_ref[...])
    lw_raw = w0_ref[...] + _dot(jnp.tanh(_dot(mixed(1), w1_ref[...])).astype(BF16), w2_ref[...])
    k = _dot(mixed(2), wk_ref[...])
    v = _dot(mixed(3), wv_ref[...])
    asig = _sigmoid(a0_ref[...] + _dot(_dot(mixed(4), a1_ref[...]).astype(BF16), a2_ref[...]))
    g = _dot(_sigmoid(_dot(mixed(5), g1_ref[...])).astype(BF16), g2_ref[...])

    log_w = -_softplus(-lw_raw) - 0.5
    lw_o[0] = -jnp.exp(log_w)
    e, et = e_ref[...], et_ref[...]
    kkr = k * kk_ref[...]
    nrm = jnp.maximum(jnp.sqrt(_head_sum(kkr * kkr, e, et)), 1e-12)
    kk = kkr * (1.0 / nrm)
    kmod = k * (1.0 + (asig - 1.0) * ka_ref[...])
    r_o[0] = r
    k_o[0] = kmod
    v_o[0] = v
    a_o[0] = -kk
    b_o[0] = kk * asig
    g_o[0] = g
    bonus_o[0] = _head_sum(r * kmod * rk_ref[...], e, et) * v
    last_o[0] = xn[tt - 1:tt, :]


def _rwkv_pre(x, shift0, p):
    B, T, D = x.shape
    tt = min(T, 256)
    nt = T // tt
    e, et = _head_onehot(D)
    row = lambda a: a.reshape(1, -1).astype(F32)
    bf = lambda a: a.astype(BF16)
    consts = [row(p["norm_mix0"]), p["rw_mix"].astype(F32), bf(p["rw_w_r"]), bf(p["rw_w_k"]), bf(p["rw_w_v"]),
              row(p["rw_w0"]), bf(p["rw_w1"]), bf(p["rw_w2"]), row(p["rw_a0"]), bf(p["rw_a1"]), bf(p["rw_a2"]),
              bf(p["rw_g1"]), bf(p["rw_g2"]), row(p["rw_k_k"]), row(p["rw_k_a"]), row(p["rw_r_k"]), e, et]
    const_specs = [pl.BlockSpec(c.shape, lambda b, t, n=c.ndim: (0,) * n) for c in consts]
    tile = pl.BlockSpec((1, tt, D), lambda b, t: (b, t, 0))
    prev8 = pl.BlockSpec((1, SUBLANES, D), lambda b, t: (b, jnp.maximum(t * (tt // SUBLANES) - 1, 0), 0))
    one = pl.BlockSpec((1, 1, D), lambda b, t: (b, 0, 0))
    big = jax.ShapeDtypeStruct((B, T, D), F32)
    outs = pl.pallas_call(
        _rwkv_pre_kernel,
        grid=(B, nt),
        in_specs=[tile, prev8, one] + const_specs,
        out_specs=[tile] * 8 + [one],
        out_shape=[big] * 8 + [jax.ShapeDtypeStruct((B, 1, D), F32)],
        compiler_params=_cparams(("parallel", "arbitrary")),
        name="rwkv_pre",
    )(x, x, shift0.reshape(B, 1, D), *consts)
    return outs


def _scan_kernel(r_ref, lw_ref, k_ref, v_ref, a_ref, b_ref, s0_ref, y_ref, sT_ref, s_scr, *, nchunk):
    C = SCAN_CHUNK
    c = pl.program_id(2)

    @pl.when(c == 0)
    def _():
        s_scr[...] = s0_ref[0, 0]

    row2 = lax.broadcasted_iota(jnp.int32, (2 * C, LANES), 0)
    lane2 = lax.broadcasted_iota(jnp.int32, (2 * C, LANES), 1)
    own = (lane2 // HEAD_DIM) == (row2 // C)
    ri = lax.broadcasted_iota(jnp.int32, (2 * C, 2 * C), 0)
    ci = lax.broadcasted_iota(jnp.int32, (2 * C, 2 * C), 1)
    strict = ci < ri
    incl = ci <= ri
    eye = (ci == ri).astype(F32)
    tri = (lax.broadcasted_iota(jnp.int32, (C, C), 1) <= lax.broadcasted_iota(jnp.int32, (C, C), 0)).astype(BF16)

    def stack(x):
        return jnp.where(own, jnp.concatenate([x, x], axis=0), 0.0)

    mm = functools.partial(_mm, na=SCAN_PIECES, nb=SCAN_PIECES)

    for g in range(nchunk):
        sl = pl.ds(g * C, C)
        r, lw, k, v, a, b = (ref[0, sl, :] for ref in (r_ref, lw_ref, k_ref, v_ref, a_ref, b_ref))
        gc = _mm(tri, lw, nb=3)
        gt = gc[C - 1:C, :]
        eneg = jnp.exp(-gc)
        efar = jnp.exp(gt - gc)
        rt = stack(r * jnp.exp(gc))
        at = stack(a * jnp.exp(gc - lw))
        bt = stack(b * eneg)
        kt = stack(k * eneg)
        bw = stack(b * efar)
        kw = stack(k * efar)
        vs = stack(v)
        L = jnp.where(strict, mm(at, bt, "nt"), 0.0)
        Lak = jnp.where(strict, mm(at, kt, "nt"), 0.0)
        RB = jnp.where(incl, mm(rt, bt, "nt"), 0.0)
        RK = jnp.where(incl, mm(rt, kt, "nt"), 0.0)
        Tinv = eye + L
        M = L
        for _ in range(int(math.log2(C)) - 1):
            M = mm(M, M)
            Tinv = Tinv + mm(M, Tinv)
        P = mm(Tinv, at)
        Q = mm(Tinv, mm(Lak, vs))
        Y0 = mm(RK, vs)
        Z0 = mm(vs, kw, "tn")
        S = s_scr[...]
        U = mm(P, S, "nt") + Q
        Y = mm(rt, S, "nt") + mm(RB, U) + Y0
        s_scr[...] = S * jnp.exp(gt) + mm(U, bw, "tn") + Z0
        y_ref[0, sl, :] = Y[:C] + Y[C:]

    @pl.when(c == pl.num_programs(2) - 1)
    def _():
        sT_ref[0, 0] = s_scr[...]


def _rwkv_scan(r, lw, k, v, a, b, s0):
    B, T, D = r.shape
    H = D // HEAD_DIM
    NP = D // LANES
    C = SCAN_CHUNK
    Tp = -(-T // C) * C
    if Tp != T:
        pad = lambda z: jnp.pad(z, ((0, 0), (0, Tp - T), (0, 0)))
        r, lw, k, v, a, b = (pad(z) for z in (r, lw, k, v, a, b))
    nchunk = 4 if Tp % (4 * C) == 0 else 1
    tb = nchunk * C
    s0p = s0.astype(F32).reshape(B, NP, 2, HEAD_DIM, HEAD_DIM)
    z = jnp.zeros_like(s0p[:, :, 0])
    s0bd = jnp.concatenate([jnp.concatenate([s0p[:, :, 0], z], axis=-1),
                            jnp.concatenate([z, s0p[:, :, 1]], axis=-1)], axis=-2)
    tile = pl.BlockSpec((1, tb, LANES), lambda bb, j, c: (bb, c, j))
    st = pl.BlockSpec((1, 1, LANES, LANES), lambda bb, j, c: (bb, j, 0, 0))
    y, sT = pl.pallas_call(
        functools.partial(_scan_kernel, nchunk=nchunk),
        grid=(B, NP, Tp // tb),
        in_specs=[tile] * 6 + [st],
        out_specs=[tile, st],
        out_shape=[jax.ShapeDtypeStruct((B, Tp, D), F32), jax.ShapeDtypeStruct((B, NP, LANES, LANES), F32)],
        scratch_shapes=[pltpu.VMEM((LANES, LANES), F32)],
        compiler_params=_cparams(("parallel", "parallel", "arbitrary")),
        name="rwkv_scan",
    )(r, lw, k, v, a, b, s0bd)
    s_fin = jnp.stack([sT[:, :, :HEAD_DIM, :HEAD_DIM], sT[:, :, HEAD_DIM:, HEAD_DIM:]], axis=2)
    return y[:, :T], s_fin.reshape(B, H, HEAD_DIM, HEAD_DIM)


def _rwkv_post_kernel(y_ref, bonus_ref, g_ref, x_ref, lnw_ref, lnb_ref, wo_ref, e_ref, et_ref, o_ref):
    e, et = e_ref[...], et_ref[...]
    y = y_ref[...]
    mu = _head_sum(y, e, et) * (1.0 / HEAD_DIM)
    d = y - mu
    var = _head_sum(d * d, e, et) * (1.0 / HEAD_DIM)
    yn = d * lax.rsqrt(var + RW_GN_EPS) * lnw_ref[...] + lnb_ref[...]
    z = ((yn + bonus_ref[...]) * g_ref[...]).astype(BF16)
    o_ref[...] = x_ref[...] + _dot(z, wo_ref[...])


def _rwkv_post(y, bonus, g, x, p):
    N, D = x.shape
    tm = min(N, 512)
    e, et = _head_onehot(D)
    consts = [p["rw_ln_w"].reshape(1, D).astype(F32), p["rw_ln_b"].reshape(1, D).astype(F32),
              p["rw_w_o"].astype(BF16), e, et]
    tile = pl.BlockSpec((tm, D), lambda i: (i, 0))
    return pl.pallas_call(
        _rwkv_post_kernel,
        grid=(N // tm,),
        in_specs=[tile] * 4 + [pl.BlockSpec(c.shape, lambda i: (0, 0)) for c in consts],
        out_specs=tile,
        out_shape=jax.ShapeDtypeStruct((N, D), F32),
        compiler_params=_cparams(("parallel",)),
        name="rwkv_post",
    )(y, bonus, g, x, *consts)


def _ffn_kernel(x_ref, nw_ref, wg_ref, wu_ref, wo_ref, o_ref, xn_scr, acc_scr):
    c = pl.program_id(1)

    @pl.when(c == 0)
    def _():
        xn_scr[...] = _rms(x_ref[...], nw_ref[...]).astype(BF16)
        acc_scr[...] = jnp.zeros_like(acc_scr)

    xn = xn_scr[...]
    gate = _dot(xn, wg_ref[...])
    up = _dot(xn, wu_ref[...])
    h = (gate * _sigmoid(gate) * up).astype(BF16)
    acc_scr[...] += _dot(h, wo_ref[...])

    @pl.when(c == pl.num_programs(1) - 1)
    def _():
        o_ref[...] = x_ref[...] + acc_scr[...]


def _ffn_chunk(f):
    for fc in (1408, 896, 512, 256, 128):
        if f % fc == 0:
            return fc
    return f


def _ffn(x, norm_w, w_in, w_out):
    N, D = x.shape
    F = w_out.shape[0]
    tm = min(N, 512)
    fc = _ffn_chunk(F)
    nfc = F // fc
    return pl.pallas_call(
        _ffn_kernel,
        grid=(N // tm, nfc),
        in_specs=[pl.BlockSpec((tm, D), lambda i, c: (i, 0)),
                  pl.BlockSpec((1, D), lambda i, c: (0, 0)),
                  pl.BlockSpec((D, fc), lambda i, c: (0, c)),
                  pl.BlockSpec((D, fc), lambda i, c: (0, nfc + c)),
                  pl.BlockSpec((fc, D), lambda i, c: (c, 0))],
        out_specs=pl.BlockSpec((tm, D), lambda i, c: (i, 0)),
        out_shape=jax.ShapeDtypeStruct((N, D), F32),
        scratch_shapes=[pltpu.VMEM((tm, D), BF16), pltpu.VMEM((tm, D), F32)],
        compiler_params=_cparams(("parallel", "arbitrary")),
        name="ffn",
    )(x, norm_w.reshape(1, D).astype(F32), w_in.astype(BF16), w_in.astype(BF16), w_out.astype(BF16))


def _kvq_kernel(x_ref, kvn_ref, qn_ref, wkv_ref, wq_ref, kg_ref, qg_ref, e_ref, et_ref,
                k_o, v_o, kb_o, vb_o, qb_o):
    e, et = e_ref[...], et_ref[...]
    x = x_ref[...]
    D = x.shape[1]
    xhat = x * lax.rsqrt(jnp.mean(x * x, axis=-1, keepdims=True) + RMS_EPS)
    kv = _dot((xhat * kvn_ref[...]).astype(BF16), wkv_ref[...])
    kraw = kv[:, :D]
    v = kv[:, D:]
    k = kraw * lax.rsqrt(_head_sum(kraw * kraw, e, et) * (1.0 / HEAD_DIM) + RMS_EPS) * kg_ref[...]
    q = _dot((xhat * qn_ref[...]).astype(BF16), wq_ref[...])
    q = q * lax.rsqrt(_head_sum(q * q, e, et) * (1.0 / HEAD_DIM) + RMS_EPS) * qg_ref[...]
    k_o[...] = k
    v_o[...] = v
    kb_o[...] = k.astype(BF16)
    vb_o[...] = v.astype(BF16)
    qb_o[...] = (q * (1.0 / math.sqrt(HEAD_DIM))).astype(BF16)


def _kvq(x, p):
    N, D = x.shape
    H = D // HEAD_DIM
    tm = min(N, 512)
    e, et = _head_onehot(D)
    row = lambda a: a.reshape(1, D).astype(F32)
    consts = [row(p["kv_norm"]), row(p["norm_mix1"]), p["kv_w"].astype(BF16), p["sb_w_q"].astype(BF16),
              row(jnp.tile(p["kv_k_norm"], H)), row(jnp.tile(p["sb_q_norm"], H)), e, et]
    tile = pl.BlockSpec((tm, D), lambda i: (i, 0))
    f = jax.ShapeDtypeStruct((N, D), F32)
    h = jax.ShapeDtypeStruct((N, D), BF16)
    return pl.pallas_call(
        _kvq_kernel,
        grid=(N // tm,),
        in_specs=[tile] + [pl.BlockSpec(c.shape, lambda i: (0, 0)) for c in consts],
        out_specs=[tile] * 5,
        out_shape=[f, f, h, h, h],
        compiler_params=_cparams(("parallel",)),
        name="kvq_proj",
    )(x, *consts)


def _sb_prompt_kernel(bias_ref, q_ref, k_ref, v_ref, o_ref, acc_scr, car_scr, *, tq):
    j = pl.program_id(1)
    i = pl.program_id(2)
    q = q_ref[0]
    lane = lax.broadcasted_iota(jnp.int32, q.shape, 1)
    qh = [jnp.where(lane < HEAD_DIM, q, jnp.zeros_like(q)), jnp.where(lane >= HEAD_DIM, q, jnp.zeros_like(q))]
    ri = lax.broadcasted_iota(jnp.int32, (tq, tq), 0)
    ci = lax.broadcasted_iota(jnp.int32, (tq, tq), 1)
    later = (ri > ci).astype(BF16)
    ucat = jnp.concatenate([later, jnp.ones((tq, LANES), BF16)], axis=1)
    causal = ci < ri
    acc_scr[...] = jnp.zeros_like(acc_scr)
    car_scr[...] = jnp.zeros_like(car_scr)

    def block(kb, masked):
        start = pl.multiple_of(kb * tq, tq)
        kblk = k_ref[0, pl.ds(start, tq), :]
        vblk = v_ref[0, pl.ds(start, tq), :]
        for h in range(2):
            z = _dot_nt(qh[h], kblk) + bias_ref[2 * j + h]
            sp = _softplus(z)
            if masked:
                sp = jnp.where(causal, sp, 0.0)
            cs = _mm(sp, ucat, na=2)
            tail = cs[:, :tq]
            carry = car_scr[h]
            arg = z - sp - tail - jnp.tile(carry, (1, tq // LANES))
            pa = jnp.exp(arg)
            if masked:
                pa = jnp.where(causal, pa, 0.0)
            acc_scr[h] += _dot(pa.astype(BF16), vblk)
            car_scr[h] = carry + cs[:, tq:]

    block(i, True)

    def body(s, _):
        block(i - 1 - s, False)
        return 0

    lax.fori_loop(0, i, body, 0)
    o_ref[0] = jnp.where(lane < HEAD_DIM, acc_scr[0], acc_scr[1]).astype(o_ref.dtype)


def _sb_prompt(q, k, v, bias):
    B, T, D = q.shape
    NP = D // LANES
    tq = min(T, 256)
    grid_spec = pltpu.PrefetchScalarGridSpec(
        num_scalar_prefetch=1,
        grid=(B, NP, T // tq),
        in_specs=[pl.BlockSpec((1, tq, LANES), lambda b, j, i, bias: (b, i, j)),
                  pl.BlockSpec((1, T, LANES), lambda b, j, i, bias: (b, 0, j)),
                  pl.BlockSpec((1, T, LANES), lambda b, j, i, bias: (b, 0, j))],
        out_specs=pl.BlockSpec((1, tq, LANES), lambda b, j, i, bias: (b, i, j)),
        scratch_shapes=[pltpu.VMEM((2, tq, LANES), F32), pltpu.VMEM((2, tq, LANES), F32)],
    )
    return pl.pallas_call(
        functools.partial(_sb_prompt_kernel, tq=tq),
        grid_spec=grid_spec,
        out_shape=jax.ShapeDtypeStruct((B, T, D), BF16),
        compiler_params=_cparams(("parallel", "parallel", "arbitrary")),
        name="sb_attn_prompt",
    )(bias.astype(F32), q, k, v)


def _sb_paged_kernel(pt_ref, q_ref, kn_ref, vn_ref, bias_ref, ck_ref, cv_ref, o_ref,
                     qrows_scr, acc_scr, car_scr, kpad_scr, vpad_scr, *, tnew, page):
    s = pl.program_id(1)
    D = q_ref.shape[2]
    H = D // HEAD_DIM
    later = (lax.broadcasted_iota(jnp.int32, (page, page), 1)
             > lax.broadcasted_iota(jnp.int32, (page, page), 0)).astype(BF16)

    def segment(kb, vb, mask):
        z = _dot_nt(kb, qrows_scr[...]) + bias_ref[...]
        sp = _softplus(z)
        if mask is not None:
            sp = jnp.where(mask, sp, 0.0)
        tail = _mm(later, sp, nb=2)
        carry = car_scr[...]
        pa = jnp.exp(z - sp - tail - carry)
        if mask is not None:
            pa = jnp.where(mask, pa, 0.0)
        acc_scr[...] += _dot_tn(pa.astype(BF16), vb)
        car_scr[...] = carry + tail[0:1, :] + sp[0:1, :]

    @pl.when(s == 0)
    def _():
        q = q_ref[0]
        rows = jnp.concatenate([q] * H, axis=0)
        rh = lax.broadcasted_iota(jnp.int32, rows.shape, 0) // tnew
        lh = lax.broadcasted_iota(jnp.int32, rows.shape, 1) // HEAD_DIM
        qrows_scr[...] = jnp.where(rh == lh, rows, 0.0).astype(BF16)
        acc_scr[...] = jnp.zeros_like(acc_scr)
        car_scr[...] = jnp.zeros_like(car_scr)
        kpad_scr[...] = jnp.zeros_like(kpad_scr)
        vpad_scr[...] = jnp.zeros_like(vpad_scr)
        kpad_scr[0:tnew, :] = kn_ref[0]
        vpad_scr[0:tnew, :] = vn_ref[0]
        key = lax.broadcasted_iota(jnp.int32, (page, LANES), 0)
        qi = lax.broadcasted_iota(jnp.int32, (page, LANES), 1) % tnew
        segment(kpad_scr[...].astype(BF16), vpad_scr[...].astype(BF16), key < qi)

    @pl.when(s > 0)
    def _():
        segment(ck_ref[0].astype(BF16), cv_ref[0].astype(BF16), None)

    @pl.when(s == pl.num_programs(1) - 1)
    def _():
        lh = lax.broadcasted_iota(jnp.int32, (tnew, D), 1) // HEAD_DIM
        out = jnp.zeros((tnew, D), F32)
        for h in range(H):
            out = out + jnp.where(lh == h, acc_scr[h * tnew:(h + 1) * tnew, :], 0.0)
        o_ref[0] = out.astype(o_ref.dtype)


def _sb_paged(q, k_new, v_new, bias, cache_k, cache_v, page_table):
    B, tnew, D = q.shape
    H = D // HEAD_DIM
    n_pool, page = cache_k.shape[0], cache_k.shape[1]
    npg = page_table.shape[1]
    assert H * tnew == LANES and page == LANES
    ck = cache_k.reshape(n_pool, page, D)
    cv = cache_v.reshape(n_pool, page, D)
    bias_row = jnp.repeat(bias.astype(F32), tnew).reshape(1, LANES)

    def page_map(b, s, pt):
        return (pt[b * npg + npg - jnp.maximum(s, 1)], 0, 0)

    seq = lambda b, s, pt: (b, 0, 0)
    grid_spec = pltpu.PrefetchScalarGridSpec(
        num_scalar_prefetch=1,
        grid=(B, npg + 1),
        in_specs=[pl.BlockSpec((1, tnew, D), seq), pl.BlockSpec((1, tnew, D), seq), pl.BlockSpec((1, tnew, D), seq),
                  pl.BlockSpec((1, LANES), lambda b, s, pt: (0, 0)),
                  pl.BlockSpec((1, page, D), page_map), pl.BlockSpec((1, page, D), page_map)],
        out_specs=pl.BlockSpec((1, tnew, D), seq),
        scratch_shapes=[pltpu.VMEM((LANES, D), BF16), pltpu.VMEM((LANES, D), F32), pltpu.VMEM((1, LANES), F32),
                        pltpu.VMEM((page, D), F32), pltpu.VMEM((page, D), F32)],
    )
    return pl.pallas_call(
        functools.partial(_sb_paged_kernel, tnew=tnew, page=page),
        grid_spec=grid_spec,
        out_shape=jax.ShapeDtypeStruct((B, tnew, D), BF16),
        compiler_params=_cparams(("parallel", "arbitrary")),
        name="sb_attn_paged",
    )(page_table.reshape(-1).astype(jnp.int32), q, k_new, v_new, bias_row, ck, cv)


def _attn_out_router_kernel(o_ref, x_ref, wo_ref, nw_ref, rt_ref, x_o, xn_o, comb_o):
    x = x_ref[...] + _dot(o_ref[...], wo_ref[...])
    xn = _rms(x, nw_ref[...])
    x_o[...] = x
    xn_o[...] = xn.astype(BF16)
    n_exp = rt_ref.shape[1]
    logits = _mm(xn, rt_ref[...])
    lane = lax.broadcasted_iota(jnp.int32, logits.shape, 1).astype(F32)
    v1 = jnp.max(logits, axis=-1, keepdims=True)
    i1 = jnp.min(jnp.where(logits == v1, lane, float(n_exp)), axis=-1, keepdims=True)
    rest = jnp.where(lane == i1, -jnp.inf, logits)
    v2 = jnp.max(rest, axis=-1, keepdims=True)
    i2 = jnp.min(jnp.where(rest == v2, lane, float(n_exp)), axis=-1, keepdims=True)
    e2 = jnp.exp(v2 - v1)
    g1 = 1.0 / (1.0 + e2)
    g2 = e2 * g1
    comb_o[...] = jnp.where(lane == i1, g1, 0.0) + jnp.where(lane == i2, g2, 0.0)


def _attn_out_router(o, x, p):
    N, D = x.shape
    tm = min(N, 512)
    E = p["moe_router"].shape[1]
    router = p["moe_router"].astype(F32)
    tile = pl.BlockSpec((tm, D), lambda i: (i, 0))
    return pl.pallas_call(
        _attn_out_router_kernel,
        grid=(N // tm,),
        in_specs=[tile, tile, pl.BlockSpec((D, D), lambda i: (0, 0)), pl.BlockSpec((1, D), lambda i: (0, 0)),
                  pl.BlockSpec((D, E), lambda i: (0, 0))],
        out_specs=[tile, tile, pl.BlockSpec((tm, E), lambda i: (i, 0))],
        out_shape=[jax.ShapeDtypeStruct((N, D), F32), jax.ShapeDtypeStruct((N, D), BF16),
                   jax.ShapeDtypeStruct((N, E), F32)],
        compiler_params=_cparams(("parallel",)),
        name="attn_out_router",
    )(o, x, p["sb_w_o"].astype(BF16), p["norm_ffn1"].reshape(1, D).astype(F32), router)


def _moe_kernel(x_ref, xn_ref, comb_ref, wg_ref, wu_ref, wo_ref, o_ref, acc_scr):
    e = pl.program_id(1)
    c = pl.program_id(2)

    @pl.when((e == 0) & (c == 0))
    def _():
        acc_scr[...] = jnp.zeros_like(acc_scr)

    xn = xn_ref[...]
    gate = _dot(xn, wg_ref[0])
    up = _dot(xn, wu_ref[0])
    h = (gate * _sigmoid(gate) * up).astype(BF16)
    comb = comb_ref[...]
    lane = lax.broadcasted_iota(jnp.int32, comb.shape, 1)
    w = jnp.sum(jnp.where(lane == e, comb, 0.0), axis=-1, keepdims=True)
    acc_scr[...] += w * _dot(h, wo_ref[0])

    @pl.when((e == pl.num_programs(1) - 1) & (c == pl.num_programs(2) - 1))
    def _():
        o_ref[...] = x_ref[...] + acc_scr[...]


def _moe(x, xn, comb, w_in, w_out):
    N, D = x.shape
    E, F = w_out.shape[0], w_out.shape[1]
    tm = min(N, 512)
    fc = _ffn_chunk(F)
    nfc = F // fc
    tile = lambda i, e, c: (i, 0)
    return pl.pallas_call(
        _moe_kernel,
        grid=(N // tm, E, nfc),
        in_specs=[pl.BlockSpec((tm, D), tile), pl.BlockSpec((tm, D), tile), pl.BlockSpec((tm, E), tile),
                  pl.BlockSpec((1, D, fc), lambda i, e, c: (e, 0, c)),
                  pl.BlockSpec((1, D, fc), lambda i, e, c: (e, 0, nfc + c)),
                  pl.BlockSpec((1, fc, D), lambda i, e, c: (e, c, 0))],
        out_specs=pl.BlockSpec((tm, D), tile),
        out_shape=jax.ShapeDtypeStruct((N, D), F32),
        scratch_shapes=[pltpu.VMEM((tm, D), F32)],
        compiler_params=_cparams(("parallel", "arbitrary", "arbitrary")),
        name="moe",
    )(x, xn, comb, w_in.astype(BF16), w_in.astype(BF16), w_out.astype(BF16))


def _decoder_group(x, shift0, wkv0, past, p):
    B, T, D = x.shape
    N = B * T
    r, lw, k, v, a, b, g, bonus, last = _rwkv_pre(x, shift0, p)
    y, s_fin = _rwkv_scan(r, lw, k, v, a, b, wkv0)
    x1 = _rwkv_post(y.reshape(N, D), bonus.reshape(N, D), g.reshape(N, D), x.reshape(N, D), p)
    x2 = _ffn(x1, p["norm_ffn0"], p["ffn_w_in"], p["ffn_w_out"])
    k_sh, v_sh, kb, vb, qb = _kvq(x2, p)
    sh3 = lambda z: z.reshape(B, T, D)
    if past is None:
        o = _sb_prompt(sh3(qb), sh3(kb), sh3(vb), p["sb_bias"])
    else:
        o = _sb_paged(sh3(qb.astype(F32)), sh3(k_sh), sh3(v_sh), p["sb_bias"], *past)
    x3, xn3, comb = _attn_out_router(o.reshape(N, D), x2, p)
    x4 = _moe(x3, xn3, comb, p["moe_w_in"], p["moe_w_out"])
    H = D // HEAD_DIM
    return (x4.reshape(B, T, D), last.reshape(1, B, D), s_fin[None], k_sh.reshape(B, T, H, HEAD_DIM),
            v_sh.reshape(B, T, H, HEAD_DIM))


def kernel(x_prompt, x_sample, cache_k, cache_v, state_wkv, state_shift, page_table, norm_mix, norm_ffn, rw_mix, rw_w_r, rw_w_k, rw_w_v, rw_w_o, rw_w0, rw_w1, rw_w2, rw_a0, rw_a1, rw_a2, rw_g1, rw_g2, rw_k_k, rw_k_a, rw_r_k, rw_ln_w, rw_ln_b, kv_norm, kv_w, kv_k_norm, sb_w_q, sb_q_norm, sb_bias, sb_w_o, ffn_w_in, ffn_w_out, moe_router, moe_w_in, moe_w_out):
    assert norm_mix.shape[0] == 2 and state_wkv.shape[0] == 1, "one RWKV layer followed by one attention layer"
    p = dict(norm_mix0=norm_mix[0], norm_mix1=norm_mix[1], norm_ffn0=norm_ffn[0], norm_ffn1=norm_ffn[1],
             rw_mix=rw_mix[0], rw_w_r=rw_w_r[0], rw_w_k=rw_w_k[0], rw_w_v=rw_w_v[0], rw_w_o=rw_w_o[0],
             rw_w0=rw_w0[0], rw_w1=rw_w1[0], rw_w2=rw_w2[0], rw_a0=rw_a0[0], rw_a1=rw_a1[0], rw_a2=rw_a2[0],
             rw_g1=rw_g1[0], rw_g2=rw_g2[0], rw_k_k=rw_k_k[0], rw_k_a=rw_k_a[0], rw_r_k=rw_r_k[0].reshape(-1),
             rw_ln_w=rw_ln_w[0], rw_ln_b=rw_ln_b[0], kv_norm=kv_norm, kv_w=kv_w, kv_k_norm=kv_k_norm,
             sb_w_q=sb_w_q[0], sb_q_norm=sb_q_norm[0], sb_bias=sb_bias[0], sb_w_o=sb_w_o[0],
             ffn_w_in=ffn_w_in[0], ffn_w_out=ffn_w_out[0], moe_router=moe_router[0], moe_w_in=moe_w_in[0],
             moe_w_out=moe_w_out[0])
    bp, _, d = x_prompt.shape
    h = d // HEAD_DIM
    y_p, shift_p, wkv_p, k_p, v_p = _decoder_group(
        x_prompt, jnp.zeros((bp, d), x_prompt.dtype), jnp.zeros((bp, h, HEAD_DIM, HEAD_DIM), x_prompt.dtype), None, p)
    y_s, shift_s, wkv_s, k_s, v_s = _decoder_group(
        x_sample, state_shift[0], state_wkv[0], (cache_k, cache_v, page_table), p)
    return (y_p, y_s, wkv_p, shift_p, k_p, v_p, wkv_s, shift_s, k_s, v_s)
```

```python
import functools
import math

import jax
import jax.numpy as jnp
from jax import lax
from jax.experimental import pallas as pl
from jax.experimental.pallas import tpu as pltpu

F32 = jnp.float32
BF16 = jnp.bfloat16

HEAD_DIM = 64
LANES = 128
SUBLANES = 8
RMS_EPS = 1e-6
RW_GN_EPS = 64e-5
TOP_K = 2
LOG2E = math.log2(math.e)
SCAN_CHUNK = 64
VMEM_LIMIT = 52 * 1024 * 1024


def _cparams(sem):
    return pltpu.CompilerParams(dimension_semantics=sem, vmem_limit_bytes=VMEM_LIMIT)


def _dot(a, b):
    return jnp.dot(a, b, preferred_element_type=F32)


def _dot_nt(a, b):
    return lax.dot_general(a, b, (((1,), (1,)), ((), ())), preferred_element_type=F32)


def _dot_tn(a, b):
    return lax.dot_general(a, b, (((0,), (0,)), ((), ())), preferred_element_type=F32)


def _split(x, n):
    parts = []
    rem = x
    for i in range(n):
        p = rem.astype(BF16)
        parts.append(p)
        if i + 1 < n:
            rem = rem - p.astype(F32)
    return parts


def _mm(a, b, kind="nn", na=3, nb=3):
    f = {"nn": _dot, "nt": _dot_nt, "tn": _dot_tn}[kind]
    ap = _split(a, na) if a.dtype != BF16 else [a]
    bp = _split(b, nb) if b.dtype != BF16 else [b]
    keep = max(len(ap), len(bp))
    out = None
    for i, x in enumerate(ap):
        for j, y in enumerate(bp):
            if i + j < keep:
                t = f(x, y)
                out = t if out is None else out + t
    return out


def _rms(x, gain):
    return x * lax.rsqrt(jnp.mean(x * x, axis=-1, keepdims=True) + RMS_EPS) * gain


def _softplus(z):
    return jnp.maximum(z, 0.0) + jnp.log(1.0 + jnp.exp(-jnp.abs(z)))


def _softplus2(z):
    return jnp.maximum(z, 0.0) + jnp.log(1.0 + jnp.exp2(-jnp.abs(z))) * LOG2E


def _sigmoid(z):
    return 1.0 / (1.0 + jnp.exp(-z))


def _head_sum(x, e, et):
    s = _mm(x, e, na=3)
    return _mm(s, et, na=3)


def _head_onehot(d):
    h = d // HEAD_DIM
    e = (jnp.arange(d)[:, None] // HEAD_DIM == jnp.arange(LANES)[None, :]).astype(BF16)
    del h
    return e, e.T


def _rwkv_pre_kernel(x_ref, xp_ref, sh_ref, nm_ref, mix_ref, wr_ref, wk_ref, wv_ref, w0_ref, w1_ref,
                     w2_ref, a0_ref, a1_ref, a2_ref, g1_ref, g2_ref, kk_ref, ka_ref, rk_ref, e_ref,
                     et_ref, r_o, lw_o, k_o, v_o, a_o, b_o, g_o, bonus_o, last_o):
    t = pl.program_id(1)
    gain = nm_ref[...]
    x = x_ref[0]
    tt = x.shape[0]
    xn = _rms(x, gain)
    prev_tile_last = _rms(xp_ref[0][SUBLANES - 1:SUBLANES, :], gain)
    first = jnp.where(t == 0, sh_ref[0], prev_tile_last)
    row = lax.broadcasted_iota(jnp.int32, xn.shape, 0)
    prev = jnp.where(row == 0, first, pltpu.roll(xn, 1, 0))
    dx = prev - xn

    def mixed(i):
        return (xn + dx * mix_ref[i:i + 1, :]).astype(BF16)

    r = _dot(mixed(0), wr_ref[...])
    lw_raw = w0_ref[...] + _dot(jnp.tanh(_dot(mixed(1), w1_ref[...])).astype(BF16), w2_ref[...])
    k = _dot(mixed(2), wk_ref[...])
    v = _dot(mixed(3), wv_ref[...])
    asig = _sigmoid(a0_ref[...] + _dot(_dot(mixed(4), a1_ref[...]).astype(BF16), a2_ref[...]))
    g = _dot(_sigmoid(_dot(mixed(5), g1_ref[...])).astype(BF16), g2_ref[...])

    log_w = -_softplus(-lw_raw) - 0.5
    lw_o[0] = -jnp.exp(log_w)
    e, et = e_ref[...], et_ref[...]
    kkr = k * kk_ref[...]
    nrm = jnp.maximum(jnp.sqrt(_head_sum(kkr * kkr, e, et)), 1e-12)
    kk = kkr * (1.0 / nrm)
    kmod = k * (1.0 + (asig - 1.0) * ka_ref[...])
    r_o[0] = r
    k_o[0] = kmod
    v_o[0] = v
    a_o[0] = -kk
    b_o[0] = kk * asig
    g_o[0] = g
    bonus_o[0] = _head_sum(r * kmod * rk_ref[...], e, et) * v
    last_o[0] = xn[tt - 1:tt, :]


def _rwkv_pre(x, shift0, p):
    B, T, D = x.shape
    tt = min(T, 256)
    nt = T // tt
    e, et = _head_onehot(D)
    row = lambda a: a.reshape(1, -1).astype(F32)
    bf = lambda a: a.astype(BF16)
    consts = [row(p["norm_mix0"]), p["rw_mix"].astype(F32), bf(p["rw_w_r"]), bf(p["rw_w_k"]), bf(p["rw_w_v"]),
              row(p["rw_w0"]), bf(p["rw_w1"]), bf(p["rw_w2"]), row(p["rw_a0"]), bf(p["rw_a1"]), bf(p["rw_a2"]),
              bf(p["rw_g1"]), bf(p["rw_g2"]), row(p["rw_k_k"]), row(p["rw_k_a"]), row(p["rw_r_k"]), e, et]
    const_specs = [pl.BlockSpec(c.shape, lambda b, t, n=c.ndim: (0,) * n) for c in consts]
    tile = pl.BlockSpec((1, tt, D), lambda b, t: (b, t, 0))
    prev8 = pl.BlockSpec((1, SUBLANES, D), lambda b, t: (b, jnp.maximum(t * (tt // SUBLANES) - 1, 0), 0))
    one = pl.BlockSpec((1, 1, D), lambda b, t: (b, 0, 0))
    big = jax.ShapeDtypeStruct((B, T, D), F32)
    outs = pl.pallas_call(
        _rwkv_pre_kernel,
        grid=(B, nt),
        in_specs=[tile, prev8, one] + const_specs,
        out_specs=[tile] * 8 + [one],
        out_shape=[big] * 8 + [jax.ShapeDtypeStruct((B, 1, D), F32)],
        compiler_params=_cparams(("parallel", "arbitrary")),
        name="rwkv_pre",
    )(x, x, shift0.reshape(B, 1, D), *consts)
    return outs


def _scan_kernel(r_ref, lw_ref, k_ref, v_ref, a_ref, b_ref, s0_ref, y_ref, sT_ref, s_scr, *, nchunk, npair):
    C = SCAN_CHUNK
    c = pl.program_id(2)

    @pl.when(c == 0)
    def _():
        s_scr[...] = s0_ref[0]

    row2 = lax.broadcasted_iota(jnp.int32, (2 * C, LANES), 0)
    lane2 = lax.broadcasted_iota(jnp.int32, (2 * C, LANES), 1)
    own = (lane2 // HEAD_DIM) == (row2 // C)
    ri = lax.broadcasted_iota(jnp.int32, (2 * C, 2 * C), 0)
    ci = lax.broadcasted_iota(jnp.int32, (2 * C, 2 * C), 1)
    strict = ci < ri
    incl = ci <= ri
    eye = (ci == ri).astype(F32)
    tri = (lax.broadcasted_iota(jnp.int32, (C, C), 1) <= lax.broadcasted_iota(jnp.int32, (C, C), 0)).astype(BF16)

    def stack(x):
        return jnp.where(own, jnp.concatenate([x, x], axis=0), 0.0).astype(BF16)

    def mm(x, y, kind="nn"):
        return _mm(x.astype(BF16), y.astype(BF16), kind)

    chains = [(p, g) for p in range(npair) for g in range(nchunk)]
    w = {}
    for ch in chains:
        p, g = ch
        sl, ln = pl.ds(g * C, C), pl.ds(p * LANES, LANES)
        r, lw, k, v, a, b = (ref[0, sl, ln] for ref in (r_ref, lw_ref, k_ref, v_ref, a_ref, b_ref))
        gc = _mm(tri, lw, nb=2)
        gt = gc[C - 1:C, :]
        eneg = jnp.exp(-gc)
        efar = jnp.exp(gt - gc)
        w[ch] = dict(rt=stack(r * jnp.exp(gc)), at=stack(a * jnp.exp(gc - lw)), bt=stack(b * eneg),
                     kt=stack(k * eneg), bw=stack(b * efar), kw=stack(k * efar), vs=stack(v), dec=jnp.exp(gt))
    for ch in chains:
        c_ = w[ch]
        G = mm(jnp.concatenate([c_["at"], c_["rt"]], axis=0), jnp.concatenate([c_["bt"], c_["kt"]], axis=0), "nt")
        c_["L"] = jnp.where(strict, G[:2 * C, :2 * C], 0.0)
        c_["Lak"] = jnp.where(strict, G[:2 * C, 2 * C:], 0.0)
        c_["RB"] = jnp.where(incl, G[2 * C:, :2 * C], 0.0).astype(BF16)
        c_["RK"] = jnp.where(incl, G[2 * C:, 2 * C:], 0.0)
    for ch in chains:
        c_ = w[ch]
        c_["Tinv"] = eye + c_["L"]
        c_["A"] = mm(c_["L"], c_["L"])
        c_["LV"] = mm(c_["Lak"], c_["vs"]).astype(BF16)
        c_["Y0"] = mm(c_["RK"], c_["vs"])
        c_["Z0"] = mm(c_["vs"], c_["kw"], "tn")
    for _ in range(int(math.log2(C)) - 2):
        for ch in chains:
            c_ = w[ch]
            both = mm(c_["A"], jnp.concatenate([c_["A"], c_["Tinv"]], axis=1))
            c_["A"] = both[:, :2 * C]
            c_["Tinv"] = c_["Tinv"] + both[:, 2 * C:]
    for ch in chains:
        c_ = w[ch]
        c_["Tinv"] = c_["Tinv"] + mm(c_["A"], c_["Tinv"])
    for ch in chains:
        c_ = w[ch]
        PQ = mm(c_["Tinv"], jnp.concatenate([c_["at"], c_["LV"]], axis=1))
        c_["P"] = PQ[:, :LANES].astype(BF16)
        c_["QT"] = PQ[:, LANES:].T
    S = [s_scr[p] for p in range(npair)]
    for g in range(nchunk):
        for p in range(npair):
            c_ = w[(p, g)]
            c_["S"] = S[p].astype(BF16)
            UT = mm(c_["S"], c_["P"], "nt") + c_["QT"]
            c_["UT"] = UT
            S[p] = S[p] * c_["dec"] + mm(UT, c_["bw"]) + c_["Z0"]
    for p in range(npair):
        s_scr[p] = S[p]
    for ch in chains:
        p, g = ch
        c_ = w[ch]
        Y = mm(c_["rt"], c_["S"], "nt") + mm(c_["RB"], c_["UT"].T) + c_["Y0"]
        y_ref[0, pl.ds(g * C, C), pl.ds(p * LANES, LANES)] = Y[:C] + Y[C:]

    @pl.when(c == pl.num_programs(2) - 1)
    def _():
        sT_ref[0] = s_scr[...]


def _rwkv_scan(r, lw, k, v, a, b, s0):
    B, T, D = r.shape
    H = D // HEAD_DIM
    NP = D // LANES
    C = SCAN_CHUNK
    Tp = -(-T // C) * C
    if Tp != T:
        pad = lambda z: jnp.pad(z, ((0, 0), (0, Tp - T), (0, 0)))
        r, lw, k, v, a, b = (pad(z) for z in (r, lw, k, v, a, b))
    nchunk = 4 if Tp % (4 * C) == 0 else 1
    npair = 2 if NP % 2 == 0 else 1
    tb = nchunk * C
    s0p = s0.astype(F32).reshape(B, NP, 2, HEAD_DIM, HEAD_DIM)
    z = jnp.zeros_like(s0p[:, :, 0])
    s0bd = jnp.concatenate([jnp.concatenate([s0p[:, :, 0], z], axis=-1),
                            jnp.concatenate([z, s0p[:, :, 1]], axis=-1)], axis=-2)
    tile = pl.BlockSpec((1, tb, npair * LANES), lambda bb, j, c: (bb, c, j))
    st = pl.BlockSpec((1, npair, LANES, LANES), lambda bb, j, c: (bb, j, 0, 0))
    y, sT = pl.pallas_call(
        functools.partial(_scan_kernel, nchunk=nchunk, npair=npair),
        grid=(B, NP // npair, Tp // tb),
        in_specs=[tile] * 6 + [st],
        out_specs=[tile, st],
        out_shape=[jax.ShapeDtypeStruct((B, Tp, D), F32), jax.ShapeDtypeStruct((B, NP, LANES, LANES), F32)],
        scratch_shapes=[pltpu.VMEM((npair, LANES, LANES), F32)],
        compiler_params=_cparams(("parallel", "parallel", "arbitrary")),
        name="rwkv_scan",
    )(r, lw, k, v, a, b, s0bd)
    s_fin = jnp.stack([sT[:, :, :HEAD_DIM, :HEAD_DIM], sT[:, :, HEAD_DIM:, HEAD_DIM:]], axis=2)
    return y[:, :T], s_fin.reshape(B, H, HEAD_DIM, HEAD_DIM)


def _rwkv_post_kernel(y_ref, bonus_ref, g_ref, x_ref, lnw_ref, lnb_ref, wo_ref, e_ref, et_ref, o_ref):
    e, et = e_ref[...], et_ref[...]
    y = y_ref[...]
    mu = _head_sum(y, e, et) * (1.0 / HEAD_DIM)
    d = y - mu
    var = _head_sum(d * d, e, et) * (1.0 / HEAD_DIM)
    yn = d * lax.rsqrt(var + RW_GN_EPS) * lnw_ref[...] + lnb_ref[...]
    z = ((yn + bonus_ref[...]) * g_ref[...]).astype(BF16)
    o_ref[...] = x_ref[...] + _dot(z, wo_ref[...])


def _rwkv_post(y, bonus, g, x, p):
    N, D = x.shape
    tm = min(N, 512)
    e, et = _head_onehot(D)
    consts = [p["rw_ln_w"].reshape(1, D).astype(F32), p["rw_ln_b"].reshape(1, D).astype(F32),
              p["rw_w_o"].astype(BF16), e, et]
    tile = pl.BlockSpec((tm, D), lambda i: (i, 0))
    return pl.pallas_call(
        _rwkv_post_kernel,
        grid=(N // tm,),
        in_specs=[tile] * 4 + [pl.BlockSpec(c.shape, lambda i: (0, 0)) for c in consts],
        out_specs=tile,
        out_shape=jax.ShapeDtypeStruct((N, D), F32),
        compiler_params=_cparams(("parallel",)),
        name="rwkv_post",
    )(y, bonus, g, x, *consts)


def _ffn_kernel(x_ref, nw_ref, wg_ref, wu_ref, wo_ref, o_ref, xn_scr, acc_scr):
    c = pl.program_id(1)

    @pl.when(c == 0)
    def _():
        xn_scr[...] = _rms(x_ref[...], nw_ref[...]).astype(BF16)
        acc_scr[...] = jnp.zeros_like(acc_scr)

    xn = xn_scr[...]
    gate = _dot(xn, wg_ref[...])
    up = _dot(xn, wu_ref[...])
    h = (gate * _sigmoid(gate) * up).astype(BF16)
    acc_scr[...] += _dot(h, wo_ref[...])

    @pl.when(c == pl.num_programs(1) - 1)
    def _():
        o_ref[...] = x_ref[...] + acc_scr[...]


def _ffn_chunk(f):
    for fc in (1408, 896, 512, 256, 128):
        if f % fc == 0:
            return fc
    return f


def _ffn(x, norm_w, w_in, w_out):
    N, D = x.shape
    F = w_out.shape[0]
    tm = min(N, 512)
    fc = _ffn_chunk(F)
    nfc = F // fc
    return pl.pallas_call(
        _ffn_kernel,
        grid=(N // tm, nfc),
        in_specs=[pl.BlockSpec((tm, D), lambda i, c: (i, 0)),
                  pl.BlockSpec((1, D), lambda i, c: (0, 0)),
                  pl.BlockSpec((D, fc), lambda i, c: (0, c)),
                  pl.BlockSpec((D, fc), lambda i, c: (0, nfc + c)),
                  pl.BlockSpec((fc, D), lambda i, c: (c, 0))],
        out_specs=pl.BlockSpec((tm, D), lambda i, c: (i, 0)),
        out_shape=jax.ShapeDtypeStruct((N, D), F32),
        scratch_shapes=[pltpu.VMEM((tm, D), BF16), pltpu.VMEM((tm, D), F32)],
        compiler_params=_cparams(("parallel", "arbitrary")),
        name="ffn",
    )(x, norm_w.reshape(1, D).astype(F32), w_in.astype(BF16), w_in.astype(BF16), w_out.astype(BF16))


def _kvq_kernel(x_ref, kvn_ref, qn_ref, wkv_ref, wq_ref, kg_ref, qg_ref, e_ref, et_ref,
                k_o, v_o, kb_o, vb_o, qb_o):
    e, et = e_ref[...], et_ref[...]
    x = x_ref[...]
    D = x.shape[1]
    xhat = x * lax.rsqrt(jnp.mean(x * x, axis=-1, keepdims=True) + RMS_EPS)
    kv = _dot((xhat * kvn_ref[...]).astype(BF16), wkv_ref[...])
    kraw = kv[:, :D]
    v = kv[:, D:]
    k = kraw * lax.rsqrt(_head_sum(kraw * kraw, e, et) * (1.0 / HEAD_DIM) + RMS_EPS) * kg_ref[...]
    q = _dot((xhat * qn_ref[...]).astype(BF16), wq_ref[...])
    q = q * lax.rsqrt(_head_sum(q * q, e, et) * (1.0 / HEAD_DIM) + RMS_EPS) * qg_ref[...]
    k_o[...] = k
    v_o[...] = v
    kb_o[...] = k.astype(BF16)
    vb_o[...] = v.astype(BF16)
    qb_o[...] = (q * (LOG2E / math.sqrt(HEAD_DIM))).astype(BF16)


def _kvq(x, p):
    N, D = x.shape
    H = D // HEAD_DIM
    tm = min(N, 512)
    e, et = _head_onehot(D)
    row = lambda a: a.reshape(1, D).astype(F32)
    consts = [row(p["kv_norm"]), row(p["norm_mix1"]), p["kv_w"].astype(BF16), p["sb_w_q"].astype(BF16),
              row(jnp.tile(p["kv_k_norm"], H)), row(jnp.tile(p["sb_q_norm"], H)), e, et]
    tile = pl.BlockSpec((tm, D), lambda i: (i, 0))
    f = jax.ShapeDtypeStruct((N, D), F32)
    h = jax.ShapeDtypeStruct((N, D), BF16)
    return pl.pallas_call(
        _kvq_kernel,
        grid=(N // tm,),
        in_specs=[tile] + [pl.BlockSpec(c.shape, lambda i: (0, 0)) for c in consts],
        out_specs=[tile] * 5,
        out_shape=[f, f, h, h, h],
        compiler_params=_cparams(("parallel",)),
        name="kvq_proj",
    )(x, *consts)


def _sb_prompt_kernel(bias_ref, q_ref, k_ref, v_ref, o_ref, acc_scr, car_scr, *, tq):
    j = pl.program_id(1)
    i = pl.program_id(2)
    q = q_ref[0]
    lane = lax.broadcasted_iota(jnp.int32, q.shape, 1)
    qh = [jnp.where(lane < HEAD_DIM, q, jnp.zeros_like(q)), jnp.where(lane >= HEAD_DIM, q, jnp.zeros_like(q))]
    ri = lax.broadcasted_iota(jnp.int32, (tq, tq), 0)
    ci = lax.broadcasted_iota(jnp.int32, (tq, tq), 1)
    later = (ri > ci).astype(BF16)
    causal = ci < ri
    rep = tq // LANES

    def kv(kb):
        start = pl.multiple_of(kb * tq, tq)
        return k_ref[0, pl.ds(start, tq), :], v_ref[0, pl.ds(start, tq), :]

    def scores(h, kblk, masked):
        z = _dot_nt(qh[h], kblk) + bias_ref[2 * j + h]
        sp = _softplus2(z)
        if masked:
            sp = jnp.where(causal, sp, 0.0)
        tail = _dot(sp.astype(BF16), later)
        total = jnp.broadcast_to(tail[:, 0:1] + sp[:, 0:1], (tq, LANES))
        return z - sp - tail, total

    def weights(part, carry, vblk, masked):
        pa = jnp.exp2(part if carry is None else part - jnp.tile(carry, (1, rep)))
        if masked:
            pa = jnp.where(causal, pa, 0.0)
        return _dot(pa.astype(BF16), vblk)

    kd, vd = kv(i)
    for h in range(2):
        part, total = scores(h, kd, True)
        acc_scr[h] = weights(part, None, vd, True)
        car_scr[h] = total

    def pair(s, _):
        kb = i - 1 - 2 * s
        ka, va = kv(kb)
        kc, vc = kv(kb - 1)
        for h in range(2):
            part_a, tot_a = scores(h, ka, False)
            part_c, tot_c = scores(h, kc, False)
            carry = car_scr[h]
            carry_c = carry + tot_a
            acc_scr[h] += weights(part_a, carry, va, False) + weights(part_c, carry_c, vc, False)
            car_scr[h] = carry_c + tot_c
        return 0

    lax.fori_loop(0, i // 2, pair, 0)

    @pl.when(i % 2 == 1)
    def _():
        k0, v0 = kv(0)
        for h in range(2):
            part, _ = scores(h, k0, False)
            acc_scr[h] += weights(part, car_scr[h], v0, False)

    o_ref[0] = jnp.where(lane < HEAD_DIM, acc_scr[0], acc_scr[1]).astype(o_ref.dtype)


def _sb_prompt(q, k, v, bias):
    B, T, D = q.shape
    NP = D // LANES
    tq = min(T, 256)
    grid_spec = pltpu.PrefetchScalarGridSpec(
        num_scalar_prefetch=1,
        grid=(B, NP, T // tq),
        in_specs=[pl.BlockSpec((1, tq, LANES), lambda b, j, i, bias: (b, i, j)),
                  pl.BlockSpec((1, T, LANES), lambda b, j, i, bias: (b, 0, j)),
                  pl.BlockSpec((1, T, LANES), lambda b, j, i, bias: (b, 0, j))],
        out_specs=pl.BlockSpec((1, tq, LANES), lambda b, j, i, bias: (b, i, j)),
        scratch_shapes=[pltpu.VMEM((2, tq, LANES), F32), pltpu.VMEM((2, tq, LANES), F32)],
    )
    return pl.pallas_call(
        functools.partial(_sb_prompt_kernel, tq=tq),
        grid_spec=grid_spec,
        out_shape=jax.ShapeDtypeStruct((B, T, D), BF16),
        compiler_params=_cparams(("parallel", "parallel", "arbitrary")),
        name="sb_attn_prompt",
    )(bias.astype(F32) * LOG2E, q, k, v)


def _sb_paged_kernel(pt_ref, q_ref, kn_ref, vn_ref, bias_ref, ck_ref, cv_ref, o_ref,
                     qf_scr, qrows_scr, acc_scr, car_scr, *, tnew, page, heads):
    s = pl.program_id(1)
    later = (lax.broadcasted_iota(jnp.int32, (page, page), 1)
             > lax.broadcasted_iota(jnp.int32, (page, page), 0)).astype(BF16)

    def segment(k_ref, v_ref, mask):
        z = bias_ref[...]
        for h in range(heads):
            kh = k_ref[0, pl.ds(h, page, stride=heads), :].astype(BF16)
            z = z + _dot_nt(kh, qrows_scr[h])
        sp = _softplus2(z)
        if mask is not None:
            sp = jnp.where(mask, sp, 0.0)
        tail = _dot(later, sp.astype(BF16))
        carry = car_scr[...]
        pa = jnp.exp2(z - sp - tail - carry)
        if mask is not None:
            pa = jnp.where(mask, pa, 0.0)
        pat = pa.T.astype(BF16)
        for h in range(heads):
            vh = v_ref[0, pl.ds(h, page, stride=heads), :].astype(BF16)
            two = (h // 2) * 2 * tnew
            res = _dot(pat[two:two + 2 * tnew, :], vh)
            acc_scr[h * tnew:(h + 1) * tnew, :] += res[(h % 2) * tnew:(h % 2 + 1) * tnew, :]
        car_scr[...] = carry + tail[0:1, :] + sp[0:1, :]

    @pl.when(s == 0)
    def _():
        q = q_ref[0]
        qf_scr[...] = jnp.zeros_like(qf_scr)
        for h in range(heads):
            qf_scr[h, h * tnew:(h + 1) * tnew, :] = q[:, h * HEAD_DIM:(h + 1) * HEAD_DIM]
        qrows_scr[...] = qf_scr[...].astype(BF16)
        acc_scr[...] = jnp.zeros_like(acc_scr)
        car_scr[...] = jnp.zeros_like(car_scr)
        key = lax.broadcasted_iota(jnp.int32, (page, LANES), 0)
        qi = lax.broadcasted_iota(jnp.int32, (page, LANES), 1) % tnew
        segment(kn_ref, vn_ref, key < qi)

    @pl.when(s > 0)
    def _():
        segment(ck_ref, cv_ref, None)

    @pl.when(s == pl.num_programs(1) - 1)
    def _():
        o_ref[0] = acc_scr[...]


def _sb_paged(q, k_new, v_new, bias, cache_k, cache_v, page_table):
    B, tnew, D = q.shape
    H = D // HEAD_DIM
    n_pool, page = cache_k.shape[0], cache_k.shape[1]
    npg = page_table.shape[1]
    assert H * tnew == LANES and page == LANES and tnew % SUBLANES == 0
    rows = page * H
    ck = cache_k.reshape(n_pool, rows, HEAD_DIM)
    cv = cache_v.reshape(n_pool, rows, HEAD_DIM)
    pad = lambda z: jnp.pad(z.reshape(B, tnew, H, HEAD_DIM),
                            ((0, 0), (0, page - tnew), (0, 0), (0, 0))).reshape(B, rows, HEAD_DIM)
    bias_row = jnp.repeat(bias.astype(F32) * LOG2E, tnew).reshape(1, LANES)

    def page_map(b, s, pt):
        return (pt[b * npg + npg - jnp.maximum(s, 1)], 0, 0)

    seq = lambda b, s, pt: (b, 0, 0)
    grid_spec = pltpu.PrefetchScalarGridSpec(
        num_scalar_prefetch=1,
        grid=(B, npg + 1),
        in_specs=[pl.BlockSpec((1, tnew, D), seq),
                  pl.BlockSpec((1, rows, HEAD_DIM), seq), pl.BlockSpec((1, rows, HEAD_DIM), seq),
                  pl.BlockSpec((1, LANES), lambda b, s, pt: (0, 0)),
                  pl.BlockSpec((1, rows, HEAD_DIM), page_map), pl.BlockSpec((1, rows, HEAD_DIM), page_map)],
        out_specs=pl.BlockSpec((1, LANES, HEAD_DIM), seq),
        scratch_shapes=[pltpu.VMEM((H, LANES, HEAD_DIM), F32), pltpu.VMEM((H, LANES, HEAD_DIM), BF16),
                        pltpu.VMEM((LANES, HEAD_DIM), F32), pltpu.VMEM((1, LANES), F32)],
    )
    o = pl.pallas_call(
        functools.partial(_sb_paged_kernel, tnew=tnew, page=page, heads=H),
        grid_spec=grid_spec,
        out_shape=jax.ShapeDtypeStruct((B, LANES, HEAD_DIM), F32),
        compiler_params=_cparams(("parallel", "arbitrary")),
        name="sb_attn_paged",
    )(page_table.reshape(-1).astype(jnp.int32), q, pad(k_new), pad(v_new), bias_row, ck, cv)
    return o.reshape(B, H, tnew, HEAD_DIM).transpose(0, 2, 1, 3).reshape(B, tnew, D).astype(BF16)


def _attn_out_router_kernel(o_ref, x_ref, wo_ref, nw_ref, rt_ref, x_o, xn_o, comb_o):
    x = x_ref[...] + _dot(o_ref[...], wo_ref[...])
    xn = _rms(x, nw_ref[...])
    x_o[...] = x
    xn_o[...] = xn.astype(BF16)
    n_exp = rt_ref.shape[1]
    logits = _mm(xn, rt_ref[...])
    lane = lax.broadcasted_iota(jnp.int32, logits.shape, 1).astype(F32)
    v1 = jnp.max(logits, axis=-1, keepdims=True)
    i1 = jnp.min(jnp.where(logits == v1, lane, float(n_exp)), axis=-1, keepdims=True)
    rest = jnp.where(lane == i1, -jnp.inf, logits)
    v2 = jnp.max(rest, axis=-1, keepdims=True)
    i2 = jnp.min(jnp.where(rest == v2, lane, float(n_exp)), axis=-1, keepdims=True)
    e2 = jnp.exp(v2 - v1)
    g1 = 1.0 / (1.0 + e2)
    g2 = e2 * g1
    comb_o[...] = jnp.where(lane == i1, g1, 0.0) + jnp.where(lane == i2, g2, 0.0)


def _attn_out_router(o, x, p):
    N, D = x.shape
    tm = min(N, 512)
    E = p["moe_router"].shape[1]
    router = p["moe_router"].astype(F32)
    tile = pl.BlockSpec((tm, D), lambda i: (i, 0))
    return pl.pallas_call(
        _attn_out_router_kernel,
        grid=(N // tm,),
        in_specs=[tile, tile, pl.BlockSpec((D, D), lambda i: (0, 0)), pl.BlockSpec((1, D), lambda i: (0, 0)),
                  pl.BlockSpec((D, E), lambda i: (0, 0))],
        out_specs=[tile, tile, pl.BlockSpec((tm, E), lambda i: (i, 0))],
        out_shape=[jax.ShapeDtypeStruct((N, D), F32), jax.ShapeDtypeStruct((N, D), BF16),
                   jax.ShapeDtypeStruct((N, E), F32)],
        compiler_params=_cparams(("parallel",)),
        name="attn_out_router",
    )(o, x, p["sb_w_o"].astype(BF16), p["norm_ffn1"].reshape(1, D).astype(F32), router)


def _moe_kernel(x_ref, xn_ref, comb_ref, wg_ref, wu_ref, wo_ref, o_ref, acc_scr):
    e = pl.program_id(1)
    c = pl.program_id(2)

    @pl.when((e == 0) & (c == 0))
    def _():
        acc_scr[...] = jnp.zeros_like(acc_scr)

    xn = xn_ref[...]
    gate = _dot(xn, wg_ref[0])
    up = _dot(xn, wu_ref[0])
    h = (gate * _sigmoid(gate) * up).astype(BF16)
    comb = comb_ref[...]
    lane = lax.broadcasted_iota(jnp.int32, comb.shape, 1)
    w = jnp.sum(jnp.where(lane == e, comb, 0.0), axis=-1, keepdims=True)
    acc_scr[...] += w * _dot(h, wo_ref[0])

    @pl.when((e == pl.num_programs(1) - 1) & (c == pl.num_programs(2) - 1))
    def _():
        o_ref[...] = x_ref[...] + acc_scr[...]


def _moe(x, xn, comb, w_in, w_out):
    N, D = x.shape
    E, F = w_out.shape[0], w_out.shape[1]
    tm = min(N, 512)
    fc = _ffn_chunk(F)
    nfc = F // fc
    tile = lambda i, e, c: (i, 0)
    return pl.pallas_call(
        _moe_kernel,
        grid=(N // tm, E, nfc),
        in_specs=[pl.BlockSpec((tm, D), tile), pl.BlockSpec((tm, D), tile), pl.BlockSpec((tm, E), tile),
                  pl.BlockSpec((1, D, fc), lambda i, e, c: (e, 0, c)),
                  pl.BlockSpec((1, D, fc), lambda i, e, c: (e, 0, nfc + c)),
                  pl.BlockSpec((1, fc, D), lambda i, e, c: (e, c, 0))],
        out_specs=pl.BlockSpec((tm, D), tile),
        out_shape=jax.ShapeDtypeStruct((N, D), F32),
        scratch_shapes=[pltpu.VMEM((tm, D), F32)],
        compiler_params=_cparams(("parallel", "arbitrary", "arbitrary")),
        name="moe",
    )(x, xn, comb, w_in.astype(BF16), w_in.astype(BF16), w_out.astype(BF16))


def _decoder_group(x, shift0, wkv0, past, p):
    B, T, D = x.shape
    N = B * T
    r, lw, k, v, a, b, g, bonus, last = _rwkv_pre(x, shift0, p)
    y, s_fin = _rwkv_scan(r, lw, k, v, a, b, wkv0)
    x1 = _rwkv_post(y.reshape(N, D), bonus.reshape(N, D), g.reshape(N, D), x.reshape(N, D), p)
    x2 = _ffn(x1, p["norm_ffn0"], p["ffn_w_in"], p["ffn_w_out"])
    k_sh, v_sh, kb, vb, qb = _kvq(x2, p)
    sh3 = lambda z: z.reshape(B, T, D)
    if past is None:
        o = _sb_prompt(sh3(qb), sh3(kb), sh3(vb), p["sb_bias"])
    else:
        o = _sb_paged(sh3(qb.astype(F32)), sh3(k_sh), sh3(v_sh), p["sb_bias"], *past)
    x3, xn3, comb = _attn_out_router(o.reshape(N, D), x2, p)
    x4 = _moe(x3, xn3, comb, p["moe_w_in"], p["moe_w_out"])
    H = D // HEAD_DIM
    return (x4.reshape(B, T, D), last.reshape(1, B, D), s_fin[None], k_sh.reshape(B, T, H, HEAD_DIM),
            v_sh.reshape(B, T, H, HEAD_DIM))


def kernel(x_prompt, x_sample, cache_k, cache_v, state_wkv, state_shift, page_table, norm_mix, norm_ffn, rw_mix, rw_w_r, rw_w_k, rw_w_v, rw_w_o, rw_w0, rw_w1, rw_w2, rw_a0, rw_a1, rw_a2, rw_g1, rw_g2, rw_k_k, rw_k_a, rw_r_k, rw_ln_w, rw_ln_b, kv_norm, kv_w, kv_k_norm, sb_w_q, sb_q_norm, sb_bias, sb_w_o, ffn_w_in, ffn_w_out, moe_router, moe_w_in, moe_w_out):
    assert norm_mix.shape[0] == 2 and state_wkv.shape[0] == 1, "one RWKV layer followed by one attention layer"
    p = dict(norm_mix0=norm_mix[0], norm_mix1=norm_mix[1], norm_ffn0=norm_ffn[0], norm_ffn1=norm_ffn[1],
             rw_mix=rw_mix[0], rw_w_r=rw_w_r[0], rw_w_k=rw_w_k[0], rw_w_v=rw_w_v[0], rw_w_o=rw_w_o[0],
             rw_w0=rw_w0[0], rw_w1=rw_w1[0], rw_w2=rw_w2[0], rw_a0=rw_a0[0], rw_a1=rw_a1[0], rw_a2=rw_a2[0],
             rw_g1=rw_g1[0], rw_g2=rw_g2[0], rw_k_k=rw_k_k[0], rw_k_a=rw_k_a[0], rw_r_k=rw_r_k[0].reshape(-1),
             rw_ln_w=rw_ln_w[0], rw_ln_b=rw_ln_b[0], kv_norm=kv_norm, kv_w=kv_w, kv_k_norm=kv_k_norm,
             sb_w_q=sb_w_q[0], sb_q_norm=sb_q_norm[0], sb_bias=sb_bias[0], sb_w_o=sb_w_o[0],
             ffn_w_in=ffn_w_in[0], ffn_w_out=ffn_w_out[0], moe_router=moe_router[0], moe_w_in=moe_w_in[0],
             moe_w_out=moe_w_out[0])
    bp, _, d = x_prompt.shape
    h = d // HEAD_DIM
    y_p, shift_p, wkv_p, k_p, v_p = _decoder_group(
        x_prompt, jnp.zeros((bp, d), x_prompt.dtype), jnp.zeros((bp, h, HEAD_DIM, HEAD_DIM), x_prompt.dtype), None, p)
    y_s, shift_s, wkv_s, k_s, v_s = _decoder_group(
        x_sample, state_shift[0], state_wkv[0], (cache_k, cache_v, page_table), p)
    return (y_p, y_s, wkv_p, shift_p, k_p, v_p, wkv_s, shift_s, k_s, v_s)
```

```python
import functools
import math

import jax
import jax.numpy as jnp
from jax import lax
from jax.experimental import pallas as pl
from jax.experimental.pallas import tpu as pltpu

F32 = jnp.float32
BF16 = jnp.bfloat16

HEAD_DIM = 64
LANES = 128
SUBLANES = 8
RMS_EPS = 1e-6
RW_GN_EPS = 64e-5
TOP_K = 2
LOG2E = math.log2(math.e)
SCAN_CHUNK = 64
VMEM_LIMIT = 52 * 1024 * 1024


def _cparams(sem):
    return pltpu.CompilerParams(dimension_semantics=sem, vmem_limit_bytes=VMEM_LIMIT)


def _dot(a, b):
    return jnp.dot(a, b, preferred_element_type=F32)


def _dot_nt(a, b):
    return lax.dot_general(a, b, (((1,), (1,)), ((), ())), preferred_element_type=F32)


def _dot_tn(a, b):
    return lax.dot_general(a, b, (((0,), (0,)), ((), ())), preferred_element_type=F32)


def _split(x, n):
    parts = []
    rem = x
    for i in range(n):
        p = rem.astype(BF16)
        parts.append(p)
        if i + 1 < n:
            rem = rem - p.astype(F32)
    return parts


def _mm(a, b, kind="nn", na=3, nb=3):
    f = {"nn": _dot, "nt": _dot_nt, "tn": _dot_tn}[kind]
    ap = _split(a, na) if a.dtype != BF16 else [a]
    bp = _split(b, nb) if b.dtype != BF16 else [b]
    keep = max(len(ap), len(bp))
    out = None
    for i, x in enumerate(ap):
        for j, y in enumerate(bp):
            if i + j < keep:
                t = f(x, y)
                out = t if out is None else out + t
    return out


def _rms(x, gain):
    return x * lax.rsqrt(jnp.mean(x * x, axis=-1, keepdims=True) + RMS_EPS) * gain


def _softplus(z):
    return jnp.maximum(z, 0.0) + jnp.log(1.0 + jnp.exp(-jnp.abs(z)))


def _softplus2(z):
    return jnp.maximum(z, 0.0) + jnp.log(1.0 + jnp.exp2(-jnp.abs(z))) * LOG2E


def _sigmoid(z):
    return 1.0 / (1.0 + jnp.exp(-z))


def _head_sum(x, e, et):
    s = _mm(x, e, na=3)
    return _mm(s, et, na=3)


def _head_onehot(d):
    h = d // HEAD_DIM
    e = (jnp.arange(d)[:, None] // HEAD_DIM == jnp.arange(LANES)[None, :]).astype(BF16)
    del h
    return e, e.T


def _rwkv_pre_kernel(x_ref, xp_ref, sh_ref, nm_ref, mix_ref, wr_ref, wk_ref, wv_ref, w0_ref, w1_ref,
                     w2_ref, a0_ref, a1_ref, a2_ref, g1_ref, g2_ref, kk_ref, ka_ref, rk_ref, e_ref,
                     et_ref, r_o, lw_o, k_o, v_o, a_o, b_o, g_o, bonus_o, last_o):
    t = pl.program_id(1)
    gain = nm_ref[...]
    x = x_ref[0]
    tt = x.shape[0]
    xn = _rms(x, gain)
    prev_tile_last = _rms(xp_ref[0][SUBLANES - 1:SUBLANES, :], gain)
    first = jnp.where(t == 0, sh_ref[0], prev_tile_last)
    row = lax.broadcasted_iota(jnp.int32, xn.shape, 0)
    prev = jnp.where(row == 0, first, pltpu.roll(xn, 1, 0))
    dx = prev - xn

    def mixed(i):
        return (xn + dx * mix_ref[i:i + 1, :]).astype(BF16)

    r = _dot(mixed(0), wr_ref[...])
    lw_raw = w0_ref[...] + _dot(jnp.tanh(_dot(mixed(1), w1_ref[...])).astype(BF16), w2_ref[...])
    k = _dot(mixed(2), wk_ref[...])
    v = _dot(mixed(3), wv_ref[...])
    asig = _sigmoid(a0_ref[...] + _dot(_dot(mixed(4), a1_ref[...]).astype(BF16), a2_ref[...]))
    g = _dot(_sigmoid(_dot(mixed(5), g1_ref[...])).astype(BF16), g2_ref[...])

    log_w = -_softplus(-lw_raw) - 0.5
    lw_o[0] = -jnp.exp(log_w)
    e, et = e_ref[...], et_ref[...]
    kkr = k * kk_ref[...]
    nrm = jnp.maximum(jnp.sqrt(_head_sum(kkr * kkr, e, et)), 1e-12)
    kk = kkr * (1.0 / nrm)
    kmod = k * (1.0 + (asig - 1.0) * ka_ref[...])
    r_o[0] = r
    k_o[0] = kmod
    v_o[0] = v
    a_o[0] = -kk
    b_o[0] = kk * asig
    g_o[0] = g
    bonus_o[0] = _head_sum(r * kmod * rk_ref[...], e, et) * v
    last_o[0] = xn[tt - 1:tt, :]


def _rwkv_pre(x, shift0, p):
    B, T, D = x.shape
    tt = min(T, 256)
    nt = T // tt
    e, et = _head_onehot(D)
    row = lambda a: a.reshape(1, -1).astype(F32)
    bf = lambda a: a.astype(BF16)
    consts = [row(p["norm_mix0"]), p["rw_mix"].astype(F32), bf(p["rw_w_r"]), bf(p["rw_w_k"]), bf(p["rw_w_v"]),
              row(p["rw_w0"]), bf(p["rw_w1"]), bf(p["rw_w2"]), row(p["rw_a0"]), bf(p["rw_a1"]), bf(p["rw_a2"]),
              bf(p["rw_g1"]), bf(p["rw_g2"]), row(p["rw_k_k"]), row(p["rw_k_a"]), row(p["rw_r_k"]), e, et]
    const_specs = [pl.BlockSpec(c.shape, lambda b, t, n=c.ndim: (0,) * n) for c in consts]
    tile = pl.BlockSpec((1, tt, D), lambda b, t: (b, t, 0))
    prev8 = pl.BlockSpec((1, SUBLANES, D), lambda b, t: (b, jnp.maximum(t * (tt // SUBLANES) - 1, 0), 0))
    one = pl.BlockSpec((1, 1, D), lambda b, t: (b, 0, 0))
    big = jax.ShapeDtypeStruct((B, T, D), F32)
    outs = pl.pallas_call(
        _rwkv_pre_kernel,
        grid=(B, nt),
        in_specs=[tile, prev8, one] + const_specs,
        out_specs=[tile] * 8 + [one],
        out_shape=[big] * 8 + [jax.ShapeDtypeStruct((B, 1, D), F32)],
        compiler_params=_cparams(("parallel", "arbitrary")),
        name="rwkv_pre",
    )(x, x, shift0.reshape(B, 1, D), *consts)
    return outs


def _scan_kernel(r_ref, lw_ref, k_ref, v_ref, a_ref, b_ref, s0_ref, y_ref, sT_ref, s_scr, *, nchunk, npair):
    C = SCAN_CHUNK
    c = pl.program_id(2)

    @pl.when(c == 0)
    def _():
        s_scr[...] = s0_ref[0]

    row2 = lax.broadcasted_iota(jnp.int32, (2 * C, LANES), 0)
    lane2 = lax.broadcasted_iota(jnp.int32, (2 * C, LANES), 1)
    own = (lane2 // HEAD_DIM) == (row2 // C)
    ri = lax.broadcasted_iota(jnp.int32, (2 * C, 2 * C), 0)
    ci = lax.broadcasted_iota(jnp.int32, (2 * C, 2 * C), 1)
    strict = ci < ri
    incl = ci <= ri
    eye = (ci == ri).astype(F32)
    tri = (lax.broadcasted_iota(jnp.int32, (C, C), 1) <= lax.broadcasted_iota(jnp.int32, (C, C), 0)).astype(BF16)

    def stack(x):
        return jnp.where(own, jnp.concatenate([x, x], axis=0), 0.0).astype(BF16)

    def mm(x, y, kind="nn"):
        return _mm(x.astype(BF16), y.astype(BF16), kind)

    chains = [(p, g) for p in range(npair) for g in range(nchunk)]
    w = {}
    for ch in chains:
        p, g = ch
        sl, ln = pl.ds(g * C, C), pl.ds(p * LANES, LANES)
        r, lw, k, v, a, b = (ref[0, sl, ln] for ref in (r_ref, lw_ref, k_ref, v_ref, a_ref, b_ref))
        gc = _mm(tri, lw, nb=2)
        gt = gc[C - 1:C, :]
        eneg = jnp.exp(-gc)
        efar = jnp.exp(gt - gc)
        w[ch] = dict(rt=stack(r * jnp.exp(gc)), at=stack(a * jnp.exp(gc - lw)), bt=stack(b * eneg),
                     kt=stack(k * eneg), bw=stack(b * efar), kw=stack(k * efar), vs=stack(v), dec=jnp.exp(gt))
    for ch in chains:
        c_ = w[ch]
        G = mm(jnp.concatenate([c_["at"], c_["rt"]], axis=0), jnp.concatenate([c_["bt"], c_["kt"]], axis=0), "nt")
        c_["L"] = jnp.where(strict, G[:2 * C, :2 * C], 0.0)
        c_["Lak"] = jnp.where(strict, G[:2 * C, 2 * C:], 0.0)
        c_["RB"] = jnp.where(incl, G[2 * C:, :2 * C], 0.0).astype(BF16)
        c_["RK"] = jnp.where(incl, G[2 * C:, 2 * C:], 0.0)
    for ch in chains:
        c_ = w[ch]
        c_["Tinv"] = eye + c_["L"]
        c_["A"] = mm(c_["L"], c_["L"])
        c_["LV"] = mm(c_["Lak"], c_["vs"]).astype(BF16)
        c_["Y0"] = mm(c_["RK"], c_["vs"])
        c_["Z0"] = mm(c_["vs"], c_["kw"], "tn")
    for _ in range(int(math.log2(C)) - 2):
        for ch in chains:
            c_ = w[ch]
            both = mm(c_["A"], jnp.concatenate([c_["A"], c_["Tinv"]], axis=1))
            c_["A"] = both[:, :2 * C]
            c_["Tinv"] = c_["Tinv"] + both[:, 2 * C:]
    for ch in chains:
        c_ = w[ch]
        c_["Tinv"] = c_["Tinv"] + mm(c_["A"], c_["Tinv"])
    for ch in chains:
        c_ = w[ch]
        PQ = mm(c_["Tinv"], jnp.concatenate([c_["at"], c_["LV"]], axis=1))
        c_["P"] = PQ[:, :LANES].astype(BF16)
        c_["QT"] = PQ[:, LANES:].T
    S = [s_scr[p] for p in range(npair)]
    for g in range(nchunk):
        for p in range(npair):
            c_ = w[(p, g)]
            c_["S"] = S[p].astype(BF16)
            UT = mm(c_["S"], c_["P"], "nt") + c_["QT"]
            c_["UT"] = UT
            S[p] = S[p] * c_["dec"] + mm(UT, c_["bw"]) + c_["Z0"]
    for p in range(npair):
        s_scr[p] = S[p]
    for ch in chains:
        p, g = ch
        c_ = w[ch]
        Y = mm(c_["rt"], c_["S"], "nt") + mm(c_["RB"], c_["UT"].T) + c_["Y0"]
        y_ref[0, pl.ds(g * C, C), pl.ds(p * LANES, LANES)] = Y[:C] + Y[C:]

    @pl.when(c == pl.num_programs(2) - 1)
    def _():
        sT_ref[0] = s_scr[...]


def _rwkv_scan(r, lw, k, v, a, b, s0):
    B, T, D = r.shape
    H = D // HEAD_DIM
    NP = D // LANES
    C = SCAN_CHUNK
    Tp = -(-T // C) * C
    if Tp != T:
        pad = lambda z: jnp.pad(z, ((0, 0), (0, Tp - T), (0, 0)))
        r, lw, k, v, a, b = (pad(z) for z in (r, lw, k, v, a, b))
    nchunk = 4 if Tp % (4 * C) == 0 else 1
    npair = 2 if NP % 2 == 0 else 1
    tb = nchunk * C
    s0p = s0.astype(F32).reshape(B, NP, 2, HEAD_DIM, HEAD_DIM)
    z = jnp.zeros_like(s0p[:, :, 0])
    s0bd = jnp.concatenate([jnp.concatenate([s0p[:, :, 0], z], axis=-1),
                            jnp.concatenate([z, s0p[:, :, 1]], axis=-1)], axis=-2)
    tile = pl.BlockSpec((1, tb, npair * LANES), lambda bb, j, c: (bb, c, j))
    st = pl.BlockSpec((1, npair, LANES, LANES), lambda bb, j, c: (bb, j, 0, 0))
    y, sT = pl.pallas_call(
        functools.partial(_scan_kernel, nchunk=nchunk, npair=npair),
        grid=(B, NP // npair, Tp // tb),
        in_specs=[tile] * 6 + [st],
        out_specs=[tile, st],
        out_shape=[jax.ShapeDtypeStruct((B, Tp, D), F32), jax.ShapeDtypeStruct((B, NP, LANES, LANES), F32)],
        scratch_shapes=[pltpu.VMEM((npair, LANES, LANES), F32)],
        compiler_params=_cparams(("parallel", "parallel", "arbitrary")),
        name="rwkv_scan",
    )(r, lw, k, v, a, b, s0bd)
    s_fin = jnp.stack([sT[:, :, :HEAD_DIM, :HEAD_DIM], sT[:, :, HEAD_DIM:, HEAD_DIM:]], axis=2)
    return y[:, :T], s_fin.reshape(B, H, HEAD_DIM, HEAD_DIM)


def _rwkv_post_kernel(y_ref, bonus_ref, g_ref, x_ref, lnw_ref, lnb_ref, wo_ref, e_ref, et_ref, o_ref):
    e, et = e_ref[...], et_ref[...]
    y = y_ref[...]
    mu = _head_sum(y, e, et) * (1.0 / HEAD_DIM)
    d = y - mu
    var = _head_sum(d * d, e, et) * (1.0 / HEAD_DIM)
    yn = d * lax.rsqrt(var + RW_GN_EPS) * lnw_ref[...] + lnb_ref[...]
    z = ((yn + bonus_ref[...]) * g_ref[...]).astype(BF16)
    o_ref[...] = x_ref[...] + _dot(z, wo_ref[...])


def _rwkv_post(y, bonus, g, x, p):
    N, D = x.shape
    tm = min(N, 512)
    e, et = _head_onehot(D)
    consts = [p["rw_ln_w"].reshape(1, D).astype(F32), p["rw_ln_b"].reshape(1, D).astype(F32),
              p["rw_w_o"].astype(BF16), e, et]
    tile = pl.BlockSpec((tm, D), lambda i: (i, 0))
    return pl.pallas_call(
        _rwkv_post_kernel,
        grid=(N // tm,),
        in_specs=[tile] * 4 + [pl.BlockSpec(c.shape, lambda i: (0, 0)) for c in consts],
        out_specs=tile,
        out_shape=jax.ShapeDtypeStruct((N, D), F32),
        compiler_params=_cparams(("parallel",)),
        name="rwkv_post",
    )(y, bonus, g, x, *consts)


def _ffn_kernel(x_ref, nw_ref, wg_ref, wu_ref, wo_ref, o_ref, xn_scr, acc_scr):
    c = pl.program_id(1)

    @pl.when(c == 0)
    def _():
        xn_scr[...] = _rms(x_ref[...], nw_ref[...]).astype(BF16)
        acc_scr[...] = jnp.zeros_like(acc_scr)

    xn = xn_scr[...]
    gate = _dot(xn, wg_ref[...])
    up = _dot(xn, wu_ref[...])
    h = (gate * _sigmoid(gate) * up).astype(BF16)
    acc_scr[...] += _dot(h, wo_ref[...])

    @pl.when(c == pl.num_programs(1) - 1)
    def _():
        o_ref[...] = x_ref[...] + acc_scr[...]


def _ffn_chunk(f):
    for fc in (1408, 896, 512, 256, 128):
        if f % fc == 0:
            return fc
    return f


def _ffn(x, norm_w, w_in, w_out):
    N, D = x.shape
    F = w_out.shape[0]
    tm = min(N, 512)
    fc = _ffn_chunk(F)
    nfc = F // fc
    return pl.pallas_call(
        _ffn_kernel,
        grid=(N // tm, nfc),
        in_specs=[pl.BlockSpec((tm, D), lambda i, c: (i, 0)),
                  pl.BlockSpec((1, D), lambda i, c: (0, 0)),
                  pl.BlockSpec((D, fc), lambda i, c: (0, c)),
                  pl.BlockSpec((D, fc), lambda i, c: (0, nfc + c)),
                  pl.BlockSpec((fc, D), lambda i, c: (c, 0))],
        out_specs=pl.BlockSpec((tm, D), lambda i, c: (i, 0)),
        out_shape=jax.ShapeDtypeStruct((N, D), F32),
        scratch_shapes=[pltpu.VMEM((tm, D), BF16), pltpu.VMEM((tm, D), F32)],
        compiler_params=_cparams(("parallel", "arbitrary")),
        name="ffn",
    )(x, norm_w.reshape(1, D).astype(F32), w_in.astype(BF16), w_in.astype(BF16), w_out.astype(BF16))


def _kvq_kernel(x_ref, kvn_ref, qn_ref, wkv_ref, wq_ref, kg_ref, qg_ref, e_ref, et_ref,
                k_o, v_o, kb_o, vb_o, qb_o):
    e, et = e_ref[...], et_ref[...]
    x = x_ref[...]
    D = x.shape[1]
    xhat = x * lax.rsqrt(jnp.mean(x * x, axis=-1, keepdims=True) + RMS_EPS)
    kv = _dot((xhat * kvn_ref[...]).astype(BF16), wkv_ref[...])
    kraw = kv[:, :D]
    v = kv[:, D:]
    k = kraw * lax.rsqrt(_head_sum(kraw * kraw, e, et) * (1.0 / HEAD_DIM) + RMS_EPS) * kg_ref[...]
    q = _dot((xhat * qn_ref[...]).astype(BF16), wq_ref[...])
    q = q * lax.rsqrt(_head_sum(q * q, e, et) * (1.0 / HEAD_DIM) + RMS_EPS) * qg_ref[...]
    k_o[...] = k
    v_o[...] = v
    kb_o[...] = k.astype(BF16)
    vb_o[...] = v.astype(BF16)
    qb_o[...] = (q * (LOG2E / math.sqrt(HEAD_DIM))).astype(BF16)


def _kvq(x, p):
    N, D = x.shape
    H = D // HEAD_DIM
    tm = min(N, 512)
    e, et = _head_onehot(D)
    row = lambda a: a.reshape(1, D).astype(F32)
    consts = [row(p["kv_norm"]), row(p["norm_mix1"]), p["kv_w"].astype(BF16), p["sb_w_q"].astype(BF16),
              row(jnp.tile(p["kv_k_norm"], H)), row(jnp.tile(p["sb_q_norm"], H)), e, et]
    tile = pl.BlockSpec((tm, D), lambda i: (i, 0))
    f = jax.ShapeDtypeStruct((N, D), F32)
    h = jax.ShapeDtypeStruct((N, D), BF16)
    return pl.pallas_call(
        _kvq_kernel,
        grid=(N // tm,),
        in_specs=[tile] + [pl.BlockSpec(c.shape, lambda i: (0, 0)) for c in consts],
        out_specs=[tile] * 5,
        out_shape=[f, f, h, h, h],
        compiler_params=_cparams(("parallel",)),
        name="kvq_proj",
    )(x, *consts)


def _sb_prompt_kernel(bias_ref, q_ref, k_ref, v_ref, o_ref, acc_scr, car_scr, *, tq):
    j = pl.program_id(1)
    i = pl.program_id(2)
    q = q_ref[0]
    lane = lax.broadcasted_iota(jnp.int32, q.shape, 1)
    qh = [jnp.where(lane < HEAD_DIM, q, jnp.zeros_like(q)), jnp.where(lane >= HEAD_DIM, q, jnp.zeros_like(q))]
    ri = lax.broadcasted_iota(jnp.int32, (tq, tq), 0)
    ci = lax.broadcasted_iota(jnp.int32, (tq, tq), 1)
    later = (ri > ci).astype(BF16)
    causal = ci < ri
    rep = tq // LANES

    def kv(kb):
        start = pl.multiple_of(kb * tq, tq)
        return k_ref[0, pl.ds(start, tq), :], v_ref[0, pl.ds(start, tq), :]

    def scores(h, kblk, masked):
        z = _dot_nt(qh[h], kblk) + bias_ref[2 * j + h]
        sp = _softplus2(z)
        if masked:
            sp = jnp.where(causal, sp, 0.0)
        tail = _dot(sp.astype(BF16), later)
        total = jnp.broadcast_to(tail[:, 0:1] + sp[:, 0:1], (tq, LANES))
        return z - sp - tail, total

    def weights(part, carry, vblk, masked):
        pa = jnp.exp2(part if carry is None else part - jnp.tile(carry, (1, rep)))
        if masked:
            pa = jnp.where(causal, pa, 0.0)
        return _dot(pa.astype(BF16), vblk)

    kd, vd = kv(i)
    for h in range(2):
        part, total = scores(h, kd, True)
        acc_scr[h] = weights(part, None, vd, True)
        car_scr[h] = total

    def pair(s, _):
        kb = i - 1 - 2 * s
        ka, va = kv(kb)
        kc, vc = kv(kb - 1)
        for h in range(2):
            part_a, tot_a = scores(h, ka, False)
            part_c, tot_c = scores(h, kc, False)
            carry = car_scr[h]
            carry_c = carry + tot_a
            acc_scr[h] += weights(part_a, carry, va, False) + weights(part_c, carry_c, vc, False)
            car_scr[h] = carry_c + tot_c
        return 0

    lax.fori_loop(0, i // 2, pair, 0)

    @pl.when(i % 2 == 1)
    def _():
        k0, v0 = kv(0)
        for h in range(2):
            part, _ = scores(h, k0, False)
            acc_scr[h] += weights(part, car_scr[h], v0, False)

    o_ref[0] = jnp.where(lane < HEAD_DIM, acc_scr[0], acc_scr[1]).astype(o_ref.dtype)


def _sb_prompt(q, k, v, bias):
    B, T, D = q.shape
    NP = D // LANES
    tq = min(T, 256)
    grid_spec = pltpu.PrefetchScalarGridSpec(
        num_scalar_prefetch=1,
        grid=(B, NP, T // tq),
        in_specs=[pl.BlockSpec((1, tq, LANES), lambda b, j, i, bias: (b, i, j)),
                  pl.BlockSpec((1, T, LANES), lambda b, j, i, bias: (b, 0, j)),
                  pl.BlockSpec((1, T, LANES), lambda b, j, i, bias: (b, 0, j))],
        out_specs=pl.BlockSpec((1, tq, LANES), lambda b, j, i, bias: (b, i, j)),
        scratch_shapes=[pltpu.VMEM((2, tq, LANES), F32), pltpu.VMEM((2, tq, LANES), F32)],
    )
    return pl.pallas_call(
        functools.partial(_sb_prompt_kernel, tq=tq),
        grid_spec=grid_spec,
        out_shape=jax.ShapeDtypeStruct((B, T, D), BF16),
        compiler_params=_cparams(("parallel", "parallel", "arbitrary")),
        name="sb_attn_prompt",
    )(bias.astype(F32) * LOG2E, q, k, v)


def _sb_paged_kernel(pt_ref, q_ref, kn_ref, vn_ref, bias_ref, ck_ref, cv_ref, o_ref,
                     acc_scr, car_scr, *, tnew, page, heads):
    s = pl.program_id(1)
    later = (lax.broadcasted_iota(jnp.int32, (page, page), 0)
             > lax.broadcasted_iota(jnp.int32, (page, page), 1)).astype(BF16)

    def segment(k_ref, v_ref, mask):
        q = q_ref[0]
        slab = 2 * tnew
        zs = []
        for h in range(heads):
            qq = q[(h // 2) * slab:(h // 2 + 1) * slab, :]
            zz = _dot(qq, k_ref[0, h].astype(BF16))
            zs.append(zz[(h % 2) * tnew:(h % 2 + 1) * tnew, :])
        z = jnp.concatenate(zs, axis=0) + bias_ref[...]
        sp = _softplus2(z)
        if mask is not None:
            sp = jnp.where(mask, sp, 0.0)
        tail = _dot(sp.astype(BF16), later)
        carry = car_scr[...]
        pa = jnp.exp2(z - sp - tail - carry)
        if mask is not None:
            pa = jnp.where(mask, pa, 0.0)
        pa = pa.astype(BF16)
        for h in range(heads):
            res = _dot_nt(pa[(h // 2) * slab:(h // 2 + 1) * slab, :], v_ref[0, h].astype(BF16))
            acc_scr[h * tnew:(h + 1) * tnew, :] += res[(h % 2) * tnew:(h % 2 + 1) * tnew, :]
        car_scr[...] = carry + jnp.broadcast_to(tail[:, 0:1] + sp[:, 0:1], carry.shape)

    @pl.when(s == 0)
    def _():
        acc_scr[...] = jnp.zeros_like(acc_scr)
        car_scr[...] = jnp.zeros_like(car_scr)
        key = lax.broadcasted_iota(jnp.int32, (LANES, page), 1)
        qi = lax.broadcasted_iota(jnp.int32, (LANES, page), 0) % tnew
        segment(kn_ref, vn_ref, key < qi)

    @pl.when(s > 0)
    def _():
        segment(ck_ref, cv_ref, None)

    @pl.when(s == pl.num_programs(1) - 1)
    def _():
        o_ref[0] = acc_scr[...]


def _sb_paged(q, k_new, v_new, bias, cache_k, cache_v, page_table):
    B, tnew, D = q.shape
    H = D // HEAD_DIM
    n_pool, page = cache_k.shape[0], cache_k.shape[1]
    npg = page_table.shape[1]
    assert H * tnew == LANES and page == LANES and tnew % SUBLANES == 0
    ck = cache_k.transpose(0, 2, 3, 1)
    cv = cache_v.transpose(0, 2, 3, 1)
    new = lambda z: jnp.pad(z.reshape(B, tnew, H, HEAD_DIM).transpose(0, 2, 3, 1),
                            ((0, 0), (0, 0), (0, 0), (0, page - tnew)))
    qrows = q.reshape(B, tnew, H, HEAD_DIM).transpose(0, 2, 1, 3).reshape(B, LANES, HEAD_DIM)
    bias_rows = jnp.broadcast_to(jnp.repeat(bias.astype(F32) * LOG2E, tnew)[:, None], (LANES, page))

    def page_map(b, s, pt):
        return (pt[b * npg + npg - jnp.maximum(s, 1)], 0, 0, 0)

    seq3 = lambda b, s, pt: (b, 0, 0)
    seq4 = lambda b, s, pt: (b, 0, 0, 0)
    pg = (1, H, HEAD_DIM, page)
    grid_spec = pltpu.PrefetchScalarGridSpec(
        num_scalar_prefetch=1,
        grid=(B, npg + 1),
        in_specs=[pl.BlockSpec((1, LANES, HEAD_DIM), seq3),
                  pl.BlockSpec(pg, seq4), pl.BlockSpec(pg, seq4),
                  pl.BlockSpec((LANES, page), lambda b, s, pt: (0, 0)),
                  pl.BlockSpec(pg, page_map), pl.BlockSpec(pg, page_map)],
        out_specs=pl.BlockSpec((1, LANES, HEAD_DIM), seq3),
        scratch_shapes=[pltpu.VMEM((LANES, HEAD_DIM), F32), pltpu.VMEM((LANES, page), F32)],
    )
    o = pl.pallas_call(
        functools.partial(_sb_paged_kernel, tnew=tnew, page=page, heads=H),
        grid_spec=grid_spec,
        out_shape=jax.ShapeDtypeStruct((B, LANES, HEAD_DIM), F32),
        compiler_params=_cparams(("parallel", "arbitrary")),
        name="sb_attn_paged",
    )(page_table.reshape(-1).astype(jnp.int32), qrows, new(k_new), new(v_new), bias_rows, ck, cv)
    return o.reshape(B, H, tnew, HEAD_DIM).transpose(0, 2, 1, 3).reshape(B, tnew, D).astype(BF16)


def _attn_out_router_kernel(o_ref, x_ref, wo_ref, nw_ref, rt_ref, x_o, xn_o, comb_o):
    x = x_ref[...] + _dot(o_ref[...], wo_ref[...])
    xn = _rms(x, nw_ref[...])
    x_o[...] = x
    xn_o[...] = xn.astype(BF16)
    n_exp = rt_ref.shape[1]
    logits = _mm(xn, rt_ref[...])
    lane = lax.broadcasted_iota(jnp.int32, logits.shape, 1).astype(F32)
    v1 = jnp.max(logits, axis=-1, keepdims=True)
    i1 = jnp.min(jnp.where(logits == v1, lane, float(n_exp)), axis=-1, keepdims=True)
    rest = jnp.where(lane == i1, -jnp.inf, logits)
    v2 = jnp.max(rest, axis=-1, keepdims=True)
    i2 = jnp.min(jnp.where(rest == v2, lane, float(n_exp)), axis=-1, keepdims=True)
    e2 = jnp.exp(v2 - v1)
    g1 = 1.0 / (1.0 + e2)
    g2 = e2 * g1
    comb_o[...] = jnp.where(lane == i1, g1, 0.0) + jnp.where(lane == i2, g2, 0.0)


def _attn_out_router(o, x, p):
    N, D = x.shape
    tm = min(N, 512)
    E = p["moe_router"].shape[1]
    router = p["moe_router"].astype(F32)
    tile = pl.BlockSpec((tm, D), lambda i: (i, 0))
    return pl.pallas_call(
        _attn_out_router_kernel,
        grid=(N // tm,),
        in_specs=[tile, tile, pl.BlockSpec((D, D), lambda i: (0, 0)), pl.BlockSpec((1, D), lambda i: (0, 0)),
                  pl.BlockSpec((D, E), lambda i: (0, 0))],
        out_specs=[tile, tile, pl.BlockSpec((tm, E), lambda i: (i, 0))],
        out_shape=[jax.ShapeDtypeStruct((N, D), F32), jax.ShapeDtypeStruct((N, D), BF16),
                   jax.ShapeDtypeStruct((N, E), F32)],
        compiler_params=_cparams(("parallel",)),
        name="attn_out_router",
    )(o, x, p["sb_w_o"].astype(BF16), p["norm_ffn1"].reshape(1, D).astype(F32), router)


def _moe_kernel(x_ref, xn_ref, comb_ref, wg_ref, wu_ref, wo_ref, o_ref, acc_scr):
    e = pl.program_id(1)
    c = pl.program_id(2)

    @pl.when((e == 0) & (c == 0))
    def _():
        acc_scr[...] = jnp.zeros_like(acc_scr)

    xn = xn_ref[...]
    gate = _dot(xn, wg_ref[0])
    up = _dot(xn, wu_ref[0])
    h = (gate * _sigmoid(gate) * up).astype(BF16)
    comb = comb_ref[...]
    lane = lax.broadcasted_iota(jnp.int32, comb.shape, 1)
    w = jnp.sum(jnp.where(lane == e, comb, 0.0), axis=-1, keepdims=True)
    acc_scr[...] += w * _dot(h, wo_ref[0])

    @pl.when((e == pl.num_programs(1) - 1) & (c == pl.num_programs(2) - 1))
    def _():
        o_ref[...] = x_ref[...] + acc_scr[...]


def _moe(x, xn, comb, w_in, w_out):
    N, D = x.shape
    E, F = w_out.shape[0], w_out.shape[1]
    tm = min(N, 512)
    fc = _ffn_chunk(F)
    nfc = F // fc
    tile = lambda i, e, c: (i, 0)
    return pl.pallas_call(
        _moe_kernel,
        grid=(N // tm, E, nfc),
        in_specs=[pl.BlockSpec((tm, D), tile), pl.BlockSpec((tm, D), tile), pl.BlockSpec((tm, E), tile),
                  pl.BlockSpec((1, D, fc), lambda i, e, c: (e, 0, c)),
                  pl.BlockSpec((1, D, fc), lambda i, e, c: (e, 0, nfc + c)),
                  pl.BlockSpec((1, fc, D), lambda i, e, c: (e, c, 0))],
        out_specs=pl.BlockSpec((tm, D), tile),
        out_shape=jax.ShapeDtypeStruct((N, D), F32),
        scratch_shapes=[pltpu.VMEM((tm, D), F32)],
        compiler_params=_cparams(("parallel", "arbitrary", "arbitrary")),
        name="moe",
    )(x, xn, comb, w_in.astype(BF16), w_in.astype(BF16), w_out.astype(BF16))


def _decoder_group(x, shift0, wkv0, past, p):
    B, T, D = x.shape
    N = B * T
    r, lw, k, v, a, b, g, bonus, last = _rwkv_pre(x, shift0, p)
    y, s_fin = _rwkv_scan(r, lw, k, v, a, b, wkv0)
    x1 = _rwkv_post(y.reshape(N, D), bonus.reshape(N, D), g.reshape(N, D), x.reshape(N, D), p)
    x2 = _ffn(x1, p["norm_ffn0"], p["ffn_w_in"], p["ffn_w_out"])
    k_sh, v_sh, kb, vb, qb = _kvq(x2, p)
    sh3 = lambda z: z.reshape(B, T, D)
    if past is None:
        o = _sb_prompt(sh3(qb), sh3(kb), sh3(vb), p["sb_bias"])
    else:
        o = _sb_paged(sh3(qb), sh3(k_sh), sh3(v_sh), p["sb_bias"], *past)
    x3, xn3, comb = _attn_out_router(o.reshape(N, D), x2, p)
    x4 = _moe(x3, xn3, comb, p["moe_w_in"], p["moe_w_out"])
    H = D // HEAD_DIM
    return (x4.reshape(B, T, D), last.reshape(1, B, D), s_fin[None], k_sh.reshape(B, T, H, HEAD_DIM),
            v_sh.reshape(B, T, H, HEAD_DIM))


def kernel(x_prompt, x_sample, cache_k, cache_v, state_wkv, state_shift, page_table, norm_mix, norm_ffn, rw_mix, rw_w_r, rw_w_k, rw_w_v, rw_w_o, rw_w0, rw_w1, rw_w2, rw_a0, rw_a1, rw_a2, rw_g1, rw_g2, rw_k_k, rw_k_a, rw_r_k, rw_ln_w, rw_ln_b, kv_norm, kv_w, kv_k_norm, sb_w_q, sb_q_norm, sb_bias, sb_w_o, ffn_w_in, ffn_w_out, moe_router, moe_w_in, moe_w_out):
    assert norm_mix.shape[0] == 2 and state_wkv.shape[0] == 1, "one RWKV layer followed by one attention layer"
    p = dict(norm_mix0=norm_mix[0], norm_mix1=norm_mix[1], norm_ffn0=norm_ffn[0], norm_ffn1=norm_ffn[1],
             rw_mix=rw_mix[0], rw_w_r=rw_w_r[0], rw_w_k=rw_w_k[0], rw_w_v=rw_w_v[0], rw_w_o=rw_w_o[0],
             rw_w0=rw_w0[0], rw_w1=rw_w1[0], rw_w2=rw_w2[0], rw_a0=rw_a0[0], rw_a1=rw_a1[0], rw_a2=rw_a2[0],
             rw_g1=rw_g1[0], rw_g2=rw_g2[0], rw_k_k=rw_k_k[0], rw_k_a=rw_k_a[0], rw_r_k=rw_r_k[0].reshape(-1),
             rw_ln_w=rw_ln_w[0], rw_ln_b=rw_ln_b[0], kv_norm=kv_norm, kv_w=kv_w, kv_k_norm=kv_k_norm,
             sb_w_q=sb_w_q[0], sb_q_norm=sb_q_norm[0], sb_bias=sb_bias[0], sb_w_o=sb_w_o[0],
             ffn_w_in=ffn_w_in[0], ffn_w_out=ffn_w_out[0], moe_router=moe_router[0], moe_w_in=moe_w_in[0],
             moe_w_out=moe_w_out[0])
    bp, _, d = x_prompt.shape
    h = d // HEAD_DIM
    y_p, shift_p, wkv_p, k_p, v_p = _decoder_group(
        x_prompt, jnp.zeros((bp, d), x_prompt.dtype), jnp.zeros((bp, h, HEAD_DIM, HEAD_DIM), x_prompt.dtype), None, p)
    y_s, shift_s, wkv_s, k_s, v_s = _decoder_group(
        x_sample, state_shift[0], state_wkv[0], (cache_k, cache_v, page_table), p)
    return (y_p, y_s, wkv_p, shift_p, k_p, v_p, wkv_s, shift_s, k_s, v_s)
```

```python
import functools
import math

import jax
import jax.numpy as jnp
from jax import lax
from jax.experimental import pallas as pl
from jax.experimental.pallas import tpu as pltpu

F32 = jnp.float32
BF16 = jnp.bfloat16

HEAD_DIM = 64
LANES = 128
SUBLANES = 8
RMS_EPS = 1e-6
RW_GN_EPS = 64e-5
TOP_K = 2
ROUTE_W = 8
LOG2E = math.log2(math.e)
SCAN_CHUNK = 64
VMEM_LIMIT = 52 * 1024 * 1024


def _cparams(sem):
    return pltpu.CompilerParams(dimension_semantics=sem, vmem_limit_bytes=VMEM_LIMIT)


def _dot(a, b):
    return jnp.dot(a, b, preferred_element_type=F32)


def _dot_nt(a, b):
    return lax.dot_general(a, b, (((1,), (1,)), ((), ())), preferred_element_type=F32)


def _dot_tn(a, b):
    return lax.dot_general(a, b, (((0,), (0,)), ((), ())), preferred_element_type=F32)


def _split(x, n):
    parts = []
    rem = x
    for i in range(n):
        p = rem.astype(BF16)
        parts.append(p)
        if i + 1 < n:
            rem = rem - p.astype(F32)
    return parts


def _mm(a, b, kind="nn", na=3, nb=3):
    f = {"nn": _dot, "nt": _dot_nt, "tn": _dot_tn}[kind]
    ap = _split(a, na) if a.dtype != BF16 else [a]
    bp = _split(b, nb) if b.dtype != BF16 else [b]
    keep = max(len(ap), len(bp))
    out = None
    for i, x in enumerate(ap):
        for j, y in enumerate(bp):
            if i + j < keep:
                t = f(x, y)
                out = t if out is None else out + t
    return out


def _rms(x, gain):
    return x * lax.rsqrt(jnp.mean(x * x, axis=-1, keepdims=True) + RMS_EPS) * gain


def _softplus(z):
    return jnp.maximum(z, 0.0) + jnp.log(1.0 + jnp.exp(-jnp.abs(z)))


def _softplus2(z):
    return jnp.maximum(z, 0.0) + jnp.log(1.0 + jnp.exp2(-jnp.abs(z))) * LOG2E


def _sigmoid(z):
    return 1.0 / (1.0 + jnp.exp(-z))


def _head_sum(x, e, et):
    s = _mm(x, e, na=3)
    return _mm(s, et, na=3)


def _head_onehot(d):
    h = d // HEAD_DIM
    e = (jnp.arange(d)[:, None] // HEAD_DIM == jnp.arange(LANES)[None, :]).astype(BF16)
    del h
    return e, e.T


def _rwkv_pre_kernel(x_ref, xp_ref, sh_ref, nm_ref, mix_ref, wr_ref, wk_ref, wv_ref, w0_ref, w1_ref,
                     w2_ref, a0_ref, a1_ref, a2_ref, g1_ref, g2_ref, kk_ref, ka_ref, rk_ref, e_ref,
                     et_ref, r_o, lw_o, k_o, v_o, a_o, b_o, g_o, bonus_o, last_o):
    t = pl.program_id(1)
    gain = nm_ref[...]
    x = x_ref[0]
    tt = x.shape[0]
    xn = _rms(x, gain)
    prev_tile_last = _rms(xp_ref[0][SUBLANES - 1:SUBLANES, :], gain)
    first = jnp.where(t == 0, sh_ref[0], prev_tile_last)
    row = lax.broadcasted_iota(jnp.int32, xn.shape, 0)
    prev = jnp.where(row == 0, first, pltpu.roll(xn, 1, 0))
    dx = prev - xn

    def mixed(i):
        return (xn + dx * mix_ref[i:i + 1, :]).astype(BF16)

    r = _dot(mixed(0), wr_ref[...])
    lw_raw = w0_ref[...] + _dot(jnp.tanh(_dot(mixed(1), w1_ref[...])).astype(BF16), w2_ref[...])
    k = _dot(mixed(2), wk_ref[...])
    v = _dot(mixed(3), wv_ref[...])
    asig = _sigmoid(a0_ref[...] + _dot(_dot(mixed(4), a1_ref[...]).astype(BF16), a2_ref[...]))
    g = _dot(_sigmoid(_dot(mixed(5), g1_ref[...])).astype(BF16), g2_ref[...])

    log_w = -_softplus(-lw_raw) - 0.5
    lw_o[0] = -jnp.exp(log_w)
    e, et = e_ref[...], et_ref[...]
    kkr = k * kk_ref[...]
    nrm = jnp.maximum(jnp.sqrt(_head_sum(kkr * kkr, e, et)), 1e-12)
    kk = kkr * (1.0 / nrm)
    kmod = k * (1.0 + (asig - 1.0) * ka_ref[...])
    r_o[0] = r
    k_o[0] = kmod
    v_o[0] = v
    a_o[0] = -kk
    b_o[0] = kk * asig
    g_o[0] = g
    bonus_o[0] = _head_sum(r * kmod * rk_ref[...], e, et) * v
    last_o[0] = xn[tt - 1:tt, :]


def _rwkv_pre(x, shift0, p):
    B, T, D = x.shape
    tt = min(T, 256)
    nt = T // tt
    e, et = _head_onehot(D)
    row = lambda a: a.reshape(1, -1).astype(F32)
    bf = lambda a: a.astype(BF16)
    consts = [row(p["norm_mix0"]), p["rw_mix"].astype(F32), bf(p["rw_w_r"]), bf(p["rw_w_k"]), bf(p["rw_w_v"]),
              row(p["rw_w0"]), bf(p["rw_w1"]), bf(p["rw_w2"]), row(p["rw_a0"]), bf(p["rw_a1"]), bf(p["rw_a2"]),
              bf(p["rw_g1"]), bf(p["rw_g2"]), row(p["rw_k_k"]), row(p["rw_k_a"]), row(p["rw_r_k"]), e, et]
    const_specs = [pl.BlockSpec(c.shape, lambda b, t, n=c.ndim: (0,) * n) for c in consts]
    tile = pl.BlockSpec((1, tt, D), lambda b, t: (b, t, 0))
    prev8 = pl.BlockSpec((1, SUBLANES, D), lambda b, t: (b, jnp.maximum(t * (tt // SUBLANES) - 1, 0), 0))
    one = pl.BlockSpec((1, 1, D), lambda b, t: (b, 0, 0))
    big = jax.ShapeDtypeStruct((B, T, D), F32)
    outs = pl.pallas_call(
        _rwkv_pre_kernel,
        grid=(B, nt),
        in_specs=[tile, prev8, one] + const_specs,
        out_specs=[tile] * 8 + [one],
        out_shape=[big] * 8 + [jax.ShapeDtypeStruct((B, 1, D), F32)],
        compiler_params=_cparams(("parallel", "arbitrary")),
        name="rwkv_pre",
    )(x, x, shift0.reshape(B, 1, D), *consts)
    return outs


def _scan_kernel(r_ref, lw_ref, k_ref, v_ref, a_ref, b_ref, s0_ref, y_ref, sT_ref, s_scr, *, nchunk, npair):
    C = SCAN_CHUNK
    c = pl.program_id(2)

    @pl.when(c == 0)
    def _():
        s_scr[...] = s0_ref[0]

    row2 = lax.broadcasted_iota(jnp.int32, (2 * C, LANES), 0)
    lane2 = lax.broadcasted_iota(jnp.int32, (2 * C, LANES), 1)
    own = (lane2 // HEAD_DIM) == (row2 // C)
    ri = lax.broadcasted_iota(jnp.int32, (2 * C, 2 * C), 0)
    ci = lax.broadcasted_iota(jnp.int32, (2 * C, 2 * C), 1)
    strict = ci < ri
    incl = ci <= ri
    eye = (ci == ri).astype(F32)
    tri = (lax.broadcasted_iota(jnp.int32, (C, C), 1) <= lax.broadcasted_iota(jnp.int32, (C, C), 0)).astype(BF16)

    def stack(x):
        return jnp.where(own, jnp.concatenate([x, x], axis=0), 0.0).astype(BF16)

    def mm(x, y, kind="nn"):
        return _mm(x.astype(BF16), y.astype(BF16), kind)

    chains = [(p, g) for p in range(npair) for g in range(nchunk)]
    w = {}
    for ch in chains:
        p, g = ch
        sl, ln = pl.ds(g * C, C), pl.ds(p * LANES, LANES)
        r, lw, k, v, a, b = (ref[0, sl, ln] for ref in (r_ref, lw_ref, k_ref, v_ref, a_ref, b_ref))
        gc = _mm(tri, lw, nb=2)
        gt = gc[C - 1:C, :]
        eneg = jnp.exp(-gc)
        efar = jnp.exp(gt - gc)
        w[ch] = dict(rt=stack(r * jnp.exp(gc)), at=stack(a * jnp.exp(gc - lw)), bt=stack(b * eneg),
                     kt=stack(k * eneg), bw=stack(b * efar), kw=stack(k * efar), vs=stack(v), dec=jnp.exp(gt))
    for ch in chains:
        c_ = w[ch]
        G = mm(jnp.concatenate([c_["at"], c_["rt"]], axis=0), jnp.concatenate([c_["bt"], c_["kt"]], axis=0), "nt")
        c_["L"] = jnp.where(strict, G[:2 * C, :2 * C], 0.0)
        c_["Lak"] = jnp.where(strict, G[:2 * C, 2 * C:], 0.0)
        c_["RB"] = jnp.where(incl, G[2 * C:, :2 * C], 0.0).astype(BF16)
        c_["RK"] = jnp.where(incl, G[2 * C:, 2 * C:], 0.0)
    for ch in chains:
        c_ = w[ch]
        c_["Tinv"] = eye + c_["L"]
        c_["A"] = mm(c_["L"], c_["L"])
        c_["LV"] = mm(c_["Lak"], c_["vs"]).astype(BF16)
        c_["Y0"] = mm(c_["RK"], c_["vs"])
        c_["Z0"] = mm(c_["vs"], c_["kw"], "tn")
    for _ in range(int(math.log2(C)) - 2):
        for ch in chains:
            c_ = w[ch]
            both = mm(c_["A"], jnp.concatenate([c_["A"], c_["Tinv"]], axis=1))
            c_["A"] = both[:, :2 * C]
            c_["Tinv"] = c_["Tinv"] + both[:, 2 * C:]
    for ch in chains:
        c_ = w[ch]
        c_["Tinv"] = c_["Tinv"] + mm(c_["A"], c_["Tinv"])
    for ch in chains:
        c_ = w[ch]
        PQ = mm(c_["Tinv"], jnp.concatenate([c_["at"], c_["LV"]], axis=1))
        c_["P"] = PQ[:, :LANES].astype(BF16)
        c_["QT"] = PQ[:, LANES:].T
    S = [s_scr[p] for p in range(npair)]
    for g in range(nchunk):
        for p in range(npair):
            c_ = w[(p, g)]
            c_["S"] = S[p].astype(BF16)
            UT = mm(c_["S"], c_["P"], "nt") + c_["QT"]
            c_["UT"] = UT
            S[p] = S[p] * c_["dec"] + mm(UT, c_["bw"]) + c_["Z0"]
    for p in range(npair):
        s_scr[p] = S[p]
    for ch in chains:
        p, g = ch
        c_ = w[ch]
        Y = mm(c_["rt"], c_["S"], "nt") + mm(c_["RB"], c_["UT"].T) + c_["Y0"]
        y_ref[0, pl.ds(g * C, C), pl.ds(p * LANES, LANES)] = Y[:C] + Y[C:]

    @pl.when(c == pl.num_programs(2) - 1)
    def _():
        sT_ref[0] = s_scr[...]


def _rwkv_scan(r, lw, k, v, a, b, s0):
    B, T, D = r.shape
    H = D // HEAD_DIM
    NP = D // LANES
    C = SCAN_CHUNK
    Tp = -(-T // C) * C
    if Tp != T:
        pad = lambda z: jnp.pad(z, ((0, 0), (0, Tp - T), (0, 0)))
        r, lw, k, v, a, b = (pad(z) for z in (r, lw, k, v, a, b))
    nchunk = 4 if Tp % (4 * C) == 0 else 1
    npair = 2 if NP % 2 == 0 else 1
    tb = nchunk * C
    s0p = s0.astype(F32).reshape(B, NP, 2, HEAD_DIM, HEAD_DIM)
    z = jnp.zeros_like(s0p[:, :, 0])
    s0bd = jnp.concatenate([jnp.concatenate([s0p[:, :, 0], z], axis=-1),
                            jnp.concatenate([z, s0p[:, :, 1]], axis=-1)], axis=-2)
    tile = pl.BlockSpec((1, tb, npair * LANES), lambda bb, j, c: (bb, c, j))
    st = pl.BlockSpec((1, npair, LANES, LANES), lambda bb, j, c: (bb, j, 0, 0))
    y, sT = pl.pallas_call(
        functools.partial(_scan_kernel, nchunk=nchunk, npair=npair),
        grid=(B, NP // npair, Tp // tb),
        in_specs=[tile] * 6 + [st],
        out_specs=[tile, st],
        out_shape=[jax.ShapeDtypeStruct((B, Tp, D), F32), jax.ShapeDtypeStruct((B, NP, LANES, LANES), F32)],
        scratch_shapes=[pltpu.VMEM((npair, LANES, LANES), F32)],
        compiler_params=_cparams(("parallel", "parallel", "arbitrary")),
        name="rwkv_scan",
    )(r, lw, k, v, a, b, s0bd)
    s_fin = jnp.stack([sT[:, :, :HEAD_DIM, :HEAD_DIM], sT[:, :, HEAD_DIM:, HEAD_DIM:]], axis=2)
    return y[:, :T], s_fin.reshape(B, H, HEAD_DIM, HEAD_DIM)


def _rwkv_post_kernel(y_ref, bonus_ref, g_ref, x_ref, lnw_ref, lnb_ref, wo_ref, e_ref, et_ref, o_ref):
    e, et = e_ref[...], et_ref[...]
    y = y_ref[...]
    mu = _head_sum(y, e, et) * (1.0 / HEAD_DIM)
    d = y - mu
    var = _head_sum(d * d, e, et) * (1.0 / HEAD_DIM)
    yn = d * lax.rsqrt(var + RW_GN_EPS) * lnw_ref[...] + lnb_ref[...]
    z = ((yn + bonus_ref[...]) * g_ref[...]).astype(BF16)
    o_ref[...] = x_ref[...] + _dot(z, wo_ref[...])


def _rwkv_post(y, bonus, g, x, p):
    N, D = x.shape
    tm = min(N, 512)
    e, et = _head_onehot(D)
    consts = [p["rw_ln_w"].reshape(1, D).astype(F32), p["rw_ln_b"].reshape(1, D).astype(F32),
              p["rw_w_o"].astype(BF16), e, et]
    tile = pl.BlockSpec((tm, D), lambda i: (i, 0))
    return pl.pallas_call(
        _rwkv_post_kernel,
        grid=(N // tm,),
        in_specs=[tile] * 4 + [pl.BlockSpec(c.shape, lambda i: (0, 0)) for c in consts],
        out_specs=tile,
        out_shape=jax.ShapeDtypeStruct((N, D), F32),
        compiler_params=_cparams(("parallel",)),
        name="rwkv_post",
    )(y, bonus, g, x, *consts)


def _ffn_kernel(x_ref, nw_ref, wg_ref, wu_ref, wo_ref, o_ref, xn_scr, acc_scr):
    c = pl.program_id(1)

    @pl.when(c == 0)
    def _():
        xn_scr[...] = _rms(x_ref[...], nw_ref[...]).astype(BF16)
        acc_scr[...] = jnp.zeros_like(acc_scr)

    xn = xn_scr[...]
    gate = _dot(xn, wg_ref[...])
    up = _dot(xn, wu_ref[...])
    h = (gate * _sigmoid(gate) * up).astype(BF16)
    acc_scr[...] += _dot(h, wo_ref[...])

    @pl.when(c == pl.num_programs(1) - 1)
    def _():
        o_ref[...] = x_ref[...] + acc_scr[...]


def _ffn_chunk(f):
    for fc in (1408, 896, 512, 256, 128):
        if f % fc == 0:
            return fc
    return f


def _ffn(x, norm_w, w_in, w_out):
    N, D = x.shape
    F = w_out.shape[0]
    tm = min(N, 512)
    fc = _ffn_chunk(F)
    nfc = F // fc
    return pl.pallas_call(
        _ffn_kernel,
        grid=(N // tm, nfc),
        in_specs=[pl.BlockSpec((tm, D), lambda i, c: (i, 0)),
                  pl.BlockSpec((1, D), lambda i, c: (0, 0)),
                  pl.BlockSpec((D, fc), lambda i, c: (0, c)),
                  pl.BlockSpec((D, fc), lambda i, c: (0, nfc + c)),
                  pl.BlockSpec((fc, D), lambda i, c: (c, 0))],
        out_specs=pl.BlockSpec((tm, D), lambda i, c: (i, 0)),
        out_shape=jax.ShapeDtypeStruct((N, D), F32),
        scratch_shapes=[pltpu.VMEM((tm, D), BF16), pltpu.VMEM((tm, D), F32)],
        compiler_params=_cparams(("parallel", "arbitrary")),
        name="ffn",
    )(x, norm_w.reshape(1, D).astype(F32), w_in.astype(BF16), w_in.astype(BF16), w_out.astype(BF16))


def _kvq_kernel(x_ref, kvn_ref, qn_ref, wkv_ref, wq_ref, kg_ref, qg_ref, e_ref, et_ref,
                k_o, v_o, kb_o, vb_o, qb_o):
    e, et = e_ref[...], et_ref[...]
    x = x_ref[...]
    D = x.shape[1]
    xhat = x * lax.rsqrt(jnp.mean(x * x, axis=-1, keepdims=True) + RMS_EPS)
    kv = _dot((xhat * kvn_ref[...]).astype(BF16), wkv_ref[...])
    kraw = kv[:, :D]
    v = kv[:, D:]
    k = kraw * lax.rsqrt(_head_sum(kraw * kraw, e, et) * (1.0 / HEAD_DIM) + RMS_EPS) * kg_ref[...]
    q = _dot((xhat * qn_ref[...]).astype(BF16), wq_ref[...])
    q = q * lax.rsqrt(_head_sum(q * q, e, et) * (1.0 / HEAD_DIM) + RMS_EPS) * qg_ref[...]
    k_o[...] = k
    v_o[...] = v
    kb_o[...] = k.astype(BF16)
    vb_o[...] = v.astype(BF16)
    qb_o[...] = (q * (LOG2E / math.sqrt(HEAD_DIM))).astype(BF16)


def _kvq(x, p):
    N, D = x.shape
    H = D // HEAD_DIM
    tm = min(N, 512)
    e, et = _head_onehot(D)
    row = lambda a: a.reshape(1, D).astype(F32)
    consts = [row(p["kv_norm"]), row(p["norm_mix1"]), p["kv_w"].astype(BF16), p["sb_w_q"].astype(BF16),
              row(jnp.tile(p["kv_k_norm"], H)), row(jnp.tile(p["sb_q_norm"], H)), e, et]
    tile = pl.BlockSpec((tm, D), lambda i: (i, 0))
    f = jax.ShapeDtypeStruct((N, D), F32)
    h = jax.ShapeDtypeStruct((N, D), BF16)
    return pl.pallas_call(
        _kvq_kernel,
        grid=(N // tm,),
        in_specs=[tile] + [pl.BlockSpec(c.shape, lambda i: (0, 0)) for c in consts],
        out_specs=[tile] * 5,
        out_shape=[f, f, h, h, h],
        compiler_params=_cparams(("parallel",)),
        name="kvq_proj",
    )(x, *consts)


def _sb_prompt_kernel(bias_ref, q_ref, k_ref, v_ref, o_ref, acc_scr, car_scr, *, tq):
    j = pl.program_id(1)
    i = pl.program_id(2)
    q = q_ref[0]
    lane = lax.broadcasted_iota(jnp.int32, q.shape, 1)
    qh = [jnp.where(lane < HEAD_DIM, q, jnp.zeros_like(q)), jnp.where(lane >= HEAD_DIM, q, jnp.zeros_like(q))]
    ri = lax.broadcasted_iota(jnp.int32, (tq, tq), 0)
    ci = lax.broadcasted_iota(jnp.int32, (tq, tq), 1)
    later = (ri > ci).astype(BF16)
    causal = ci < ri
    rep = tq // LANES

    def kv(kb):
        start = pl.multiple_of(kb * tq, tq)
        return k_ref[0, pl.ds(start, tq), :], v_ref[0, pl.ds(start, tq), :]

    def logits(h, kblk):
        return _dot_nt(qh[h], kblk) + bias_ref[2 * j + h]

    def sums(z, masked):
        sp = _softplus2(z)
        if masked:
            sp = jnp.where(causal, sp, 0.0)
        tail = _dot(sp.astype(BF16), later)
        total = jnp.broadcast_to(tail[:, 0:1] + sp[:, 0:1], (tq, LANES))
        return z - sp - tail, total

    def scores(h, kblk, masked):
        return sums(logits(h, kblk), masked)

    def weights(part, carry, vblk, masked):
        pa = jnp.exp2(part if carry is None else part - jnp.tile(carry, (1, rep)))
        if masked:
            pa = jnp.where(causal, pa, 0.0)
        return _dot(pa.astype(BF16), vblk)

    kd, vd = kv(i)
    for h in range(2):
        part, total = scores(h, kd, True)
        acc_scr[h] = weights(part, None, vd, True)
        car_scr[h] = total

    def pair(s, _):
        kb = i - 1 - 2 * s
        ka, va = kv(kb)
        kc, vc = kv(kb - 1)
        zs = [(logits(h, ka), logits(h, kc)) for h in range(2)]
        ps = [(sums(za, False), sums(zc, False)) for za, zc in zs]
        for h in range(2):
            (part_a, tot_a), (part_c, tot_c) = ps[h]
            carry = car_scr[h]
            carry_c = carry + tot_a
            acc_scr[h] += weights(part_a, carry, va, False) + weights(part_c, carry_c, vc, False)
            car_scr[h] = carry_c + tot_c
        return 0

    lax.fori_loop(0, i // 2, pair, 0)

    @pl.when(i % 2 == 1)
    def _():
        k0, v0 = kv(0)
        for h in range(2):
            part, _ = scores(h, k0, False)
            acc_scr[h] += weights(part, car_scr[h], v0, False)

    o_ref[0] = jnp.where(lane < HEAD_DIM, acc_scr[0], acc_scr[1]).astype(o_ref.dtype)


def _sb_prompt(q, k, v, bias):
    B, T, D = q.shape
    NP = D // LANES
    tq = min(T, 256)
    grid_spec = pltpu.PrefetchScalarGridSpec(
        num_scalar_prefetch=1,
        grid=(B, NP, T // tq),
        in_specs=[pl.BlockSpec((1, tq, LANES), lambda b, j, i, bias: (b, i, j)),
                  pl.BlockSpec((1, T, LANES), lambda b, j, i, bias: (b, 0, j)),
                  pl.BlockSpec((1, T, LANES), lambda b, j, i, bias: (b, 0, j))],
        out_specs=pl.BlockSpec((1, tq, LANES), lambda b, j, i, bias: (b, i, j)),
        scratch_shapes=[pltpu.VMEM((2, tq, LANES), F32), pltpu.VMEM((2, tq, LANES), F32)],
    )
    return pl.pallas_call(
        functools.partial(_sb_prompt_kernel, tq=tq),
        grid_spec=grid_spec,
        out_shape=jax.ShapeDtypeStruct((B, T, D), BF16),
        compiler_params=_cparams(("parallel", "parallel", "arbitrary")),
        name="sb_attn_prompt",
    )(bias.astype(F32) * LOG2E, q, k, v)


def _sb_paged_kernel(pt_ref, q_ref, kn_ref, vn_ref, bias_ref, ck_ref, cv_ref, o_ref,
                     acc_scr, car_scr, *, tnew, page, heads):
    s = pl.program_id(1)
    later = (lax.broadcasted_iota(jnp.int32, (page, page), 0)
             > lax.broadcasted_iota(jnp.int32, (page, page), 1)).astype(BF16)

    def segment(k_ref, v_ref, mask):
        q = q_ref[0]
        slab = 2 * tnew
        zs = []
        for h in range(heads):
            qq = q[(h // 2) * slab:(h // 2 + 1) * slab, :]
            zz = _dot(qq, k_ref[0, h].astype(BF16))
            zs.append(zz[(h % 2) * tnew:(h % 2 + 1) * tnew, :])
        z = jnp.concatenate(zs, axis=0) + bias_ref[...]
        sp = _softplus2(z)
        if mask is not None:
            sp = jnp.where(mask, sp, 0.0)
        tail = _dot(sp.astype(BF16), later)
        carry = car_scr[...]
        pa = jnp.exp2(z - sp - tail - carry)
        if mask is not None:
            pa = jnp.where(mask, pa, 0.0)
        pa = pa.astype(BF16)
        for h in range(heads):
            res = _dot_nt(pa[(h // 2) * slab:(h // 2 + 1) * slab, :], v_ref[0, h].astype(BF16))
            acc_scr[h * tnew:(h + 1) * tnew, :] += res[(h % 2) * tnew:(h % 2 + 1) * tnew, :]
        car_scr[...] = carry + jnp.broadcast_to(tail[:, 0:1] + sp[:, 0:1], carry.shape)

    @pl.when(s == 0)
    def _():
        acc_scr[...] = jnp.zeros_like(acc_scr)
        car_scr[...] = jnp.zeros_like(car_scr)
        key = lax.broadcasted_iota(jnp.int32, (LANES, page), 1)
        qi = lax.broadcasted_iota(jnp.int32, (LANES, page), 0) % tnew
        segment(kn_ref, vn_ref, key < qi)

    @pl.when(s > 0)
    def _():
        segment(ck_ref, cv_ref, None)

    @pl.when(s == pl.num_programs(1) - 1)
    def _():
        o_ref[0] = acc_scr[...]


def _sb_paged(q, k_new, v_new, bias, cache_k, cache_v, page_table):
    B, tnew, D = q.shape
    H = D // HEAD_DIM
    n_pool, page = cache_k.shape[0], cache_k.shape[1]
    npg = page_table.shape[1]
    assert H * tnew == LANES and page == LANES and tnew % SUBLANES == 0
    ck = cache_k.transpose(0, 2, 3, 1)
    cv = cache_v.transpose(0, 2, 3, 1)
    new = lambda z: jnp.pad(z.reshape(B, tnew, H, HEAD_DIM).transpose(0, 2, 3, 1),
                            ((0, 0), (0, 0), (0, 0), (0, page - tnew)))
    qrows = q.reshape(B, tnew, H, HEAD_DIM).transpose(0, 2, 1, 3).reshape(B, LANES, HEAD_DIM)
    bias_rows = jnp.broadcast_to(jnp.repeat(bias.astype(F32) * LOG2E, tnew)[:, None], (LANES, page))

    def page_map(b, s, pt):
        return (pt[b * npg + npg - jnp.maximum(s, 1)], 0, 0, 0)

    seq3 = lambda b, s, pt: (b, 0, 0)
    seq4 = lambda b, s, pt: (b, 0, 0, 0)
    pg = (1, H, HEAD_DIM, page)
    grid_spec = pltpu.PrefetchScalarGridSpec(
        num_scalar_prefetch=1,
        grid=(B, npg + 1),
        in_specs=[pl.BlockSpec((1, LANES, HEAD_DIM), seq3),
                  pl.BlockSpec(pg, seq4), pl.BlockSpec(pg, seq4),
                  pl.BlockSpec((LANES, page), lambda b, s, pt: (0, 0)),
                  pl.BlockSpec(pg, page_map), pl.BlockSpec(pg, page_map)],
        out_specs=pl.BlockSpec((1, LANES, HEAD_DIM), seq3),
        scratch_shapes=[pltpu.VMEM((LANES, HEAD_DIM), F32), pltpu.VMEM((LANES, page), F32)],
    )
    o = pl.pallas_call(
        functools.partial(_sb_paged_kernel, tnew=tnew, page=page, heads=H),
        grid_spec=grid_spec,
        out_shape=jax.ShapeDtypeStruct((B, LANES, HEAD_DIM), F32),
        compiler_params=_cparams(("parallel", "arbitrary")),
        name="sb_attn_paged",
    )(page_table.reshape(-1).astype(jnp.int32), qrows, new(k_new), new(v_new), bias_rows, ck, cv)
    return o.reshape(B, H, tnew, HEAD_DIM).transpose(0, 2, 1, 3).reshape(B, tnew, D).astype(BF16)


def _attn_out_router_kernel(o_ref, x_ref, wo_ref, nw_ref, rt_ref, x_o, xn_o, route_o):
    x = x_ref[...] + _dot(o_ref[...], wo_ref[...])
    xn = _rms(x, nw_ref[...])
    x_o[...] = x
    xn_o[...] = xn
    n_exp = rt_ref.shape[1]
    logits = _mm(xn, rt_ref[...])
    lane = lax.broadcasted_iota(jnp.int32, logits.shape, 1).astype(F32)
    v1 = jnp.max(logits, axis=-1, keepdims=True)
    i1 = jnp.min(jnp.where(logits == v1, lane, float(n_exp)), axis=-1, keepdims=True)
    rest = jnp.where(lane == i1, -jnp.inf, logits)
    v2 = jnp.max(rest, axis=-1, keepdims=True)
    i2 = jnp.min(jnp.where(rest == v2, lane, float(n_exp)), axis=-1, keepdims=True)
    e2 = jnp.exp(v2 - v1)
    g1 = 1.0 / (1.0 + e2)
    g2 = e2 * g1
    col = lax.broadcasted_iota(jnp.int32, (x.shape[0], ROUTE_W), 1)
    route_o[...] = (jnp.where(col == 0, i1, 0.0) + jnp.where(col == 1, i2, 0.0)
                    + jnp.where(col == 2, g1, 0.0) + jnp.where(col == 3, g2, 0.0))


def _attn_out_router(o, x, p):
    N, D = x.shape
    tm = min(N, 512)
    E = p["moe_router"].shape[1]
    router = p["moe_router"].astype(F32)
    tile = pl.BlockSpec((tm, D), lambda i: (i, 0))
    return pl.pallas_call(
        _attn_out_router_kernel,
        grid=(N // tm,),
        in_specs=[tile, tile, pl.BlockSpec((D, D), lambda i: (0, 0)), pl.BlockSpec((1, D), lambda i: (0, 0)),
                  pl.BlockSpec((D, E), lambda i: (0, 0))],
        out_specs=[tile, tile, pl.BlockSpec((tm, ROUTE_W), lambda i: (i, 0))],
        out_shape=[jax.ShapeDtypeStruct((N, D), F32), jax.ShapeDtypeStruct((N, D), F32),
                   jax.ShapeDtypeStruct((N, ROUTE_W), F32)],
        compiler_params=_cparams(("parallel",)),
        name="attn_out_router",
    )(o, x, p["sb_w_o"].astype(BF16), p["norm_ffn1"].reshape(1, D).astype(F32), router)


def _moe_plan(route, n_exp, tm, n_rows):
    n = route.shape[0]
    eid = jnp.concatenate([route[:, 0], route[:, 1]]).astype(jnp.int32)
    onehot = (eid[:, None] == jnp.arange(n_exp, dtype=jnp.int32)[None, :]).astype(jnp.int32)
    cum = jnp.cumsum(onehot, axis=0)
    counts = cum[-1]
    rank = jnp.sum(onehot * (cum - 1), axis=1)
    padded = (counts + tm - 1) // tm * tm
    ends = jnp.cumsum(padded)
    dest = (ends - padded)[eid] + rank
    src = jnp.zeros((n_rows,), jnp.int32).at[dest].set(jnp.tile(jnp.arange(n, dtype=jnp.int32), TOP_K))
    tile_start = jnp.arange(n_rows // tm, dtype=jnp.int32) * tm
    tile_exp = jnp.minimum(jnp.searchsorted(ends, tile_start, side="right"), n_exp - 1).astype(jnp.int32)
    active = (tile_start < ends[-1]).astype(jnp.int32)
    return src, dest, tile_exp, active


def _row_copy(src_hbm, row, dst, slot, sem):
    return pltpu.make_async_copy(src_hbm.at[pl.ds(row, 1), :], dst.at[pl.ds(slot, 1), :], sem)


def _moe_gather_kernel(src_ref, x_hbm, o_ref, sem):
    rows = o_ref.shape[0]

    def issue(r, carry):
        _row_copy(x_hbm, src_ref[r // LANES, r % LANES], o_ref, r, sem).start()
        return carry

    lax.fori_loop(0, rows, issue, 0, unroll=8)
    pltpu.make_async_copy(x_hbm.at[pl.ds(0, rows), :], o_ref, sem).wait()


def _moe_gather(xn, src, gt):
    n_rows = src.shape[0]
    D = xn.shape[1]
    return pl.pallas_call(
        _moe_gather_kernel,
        grid=(n_rows // gt,),
        in_specs=[pl.BlockSpec((gt // LANES, LANES), lambda i: (i, 0), memory_space=pltpu.SMEM),
                  pl.BlockSpec(memory_space=pl.ANY)],
        out_specs=pl.BlockSpec((gt, D), lambda i: (i, 0)),
        out_shape=jax.ShapeDtypeStruct((n_rows, D), xn.dtype),
        scratch_shapes=[pltpu.SemaphoreType.DMA(())],
        compiler_params=_cparams(("arbitrary",)),
        name="moe_gather",
    )(src.reshape(n_rows // LANES, LANES), xn)


def _moe_experts_kernel(te_ref, act_ref, xs_ref, wg_ref, wu_ref, wo_ref, o_ref, acc_scr):
    i = pl.program_id(0)
    c = pl.program_id(1)

    @pl.when(c == 0)
    def _():
        acc_scr[...] = jnp.zeros_like(acc_scr)

    @pl.when(act_ref[i] > 0)
    def _():
        xs = xs_ref[...].astype(BF16)
        gate = _dot(xs, wg_ref[0])
        up = _dot(xs, wu_ref[0])
        h = (gate * _sigmoid(gate) * up).astype(BF16)
        acc_scr[...] += _dot(h, wo_ref[0])

    @pl.when(c == pl.num_programs(1) - 1)
    def _():
        o_ref[...] = acc_scr[...]


def _moe_experts(xs, tile_exp, active, w_in, w_out, tm):
    n_rows, D = xs.shape
    F = w_out.shape[1]
    fc = _ffn_chunk(F)
    nfc = F // fc
    wi = w_in.astype(BF16)
    grid_spec = pltpu.PrefetchScalarGridSpec(
        num_scalar_prefetch=2,
        grid=(n_rows // tm, nfc),
        in_specs=[pl.BlockSpec((tm, D), lambda i, c, te, act: (i, 0)),
                  pl.BlockSpec((1, D, fc), lambda i, c, te, act: (te[i], 0, c * act[i])),
                  pl.BlockSpec((1, D, fc), lambda i, c, te, act: (te[i], 0, nfc + c * act[i])),
                  pl.BlockSpec((1, fc, D), lambda i, c, te, act: (te[i], c * act[i], 0))],
        out_specs=pl.BlockSpec((tm, D), lambda i, c, te, act: (i, 0)),
        scratch_shapes=[pltpu.VMEM((tm, D), F32)],
    )
    return pl.pallas_call(
        _moe_experts_kernel,
        grid_spec=grid_spec,
        out_shape=jax.ShapeDtypeStruct((n_rows, D), F32),
        compiler_params=_cparams(("parallel", "arbitrary")),
        name="moe_experts",
    )(tile_exp, active, xs, wi, wi, w_out.astype(BF16))


def _moe_combine_kernel(pos_ref, x_ref, route_ref, ys_hbm, o_ref, buf, sem):
    tm = x_ref.shape[0]

    def issue(t, carry):
        for k in range(TOP_K):
            _row_copy(ys_hbm, pos_ref[k, t // LANES, t % LANES], buf.at[k], t, sem).start()
        return carry

    lax.fori_loop(0, tm, issue, 0, unroll=4)
    for k in range(TOP_K):
        pltpu.make_async_copy(ys_hbm.at[pl.ds(0, tm), :], buf.at[k], sem).wait()
    route = route_ref[...]
    o_ref[...] = x_ref[...] + route[:, 2:3] * buf[0] + route[:, 3:4] * buf[1]


def _moe_combine(x, route, ys, dest, ct):
    N, D = x.shape
    pos = dest.reshape(TOP_K, N // LANES, LANES)
    return pl.pallas_call(
        _moe_combine_kernel,
        grid=(N // ct,),
        in_specs=[pl.BlockSpec((TOP_K, ct // LANES, LANES), lambda i: (0, i, 0), memory_space=pltpu.SMEM),
                  pl.BlockSpec((ct, D), lambda i: (i, 0)),
                  pl.BlockSpec((ct, ROUTE_W), lambda i: (i, 0)),
                  pl.BlockSpec(memory_space=pl.ANY)],
        out_specs=pl.BlockSpec((ct, D), lambda i: (i, 0)),
        out_shape=jax.ShapeDtypeStruct((N, D), F32),
        scratch_shapes=[pltpu.VMEM((TOP_K, ct, D), F32), pltpu.SemaphoreType.DMA(())],
        compiler_params=_cparams(("arbitrary",)),
        name="moe_combine",
    )(pos, x, route, ys)


def _moe(x, xn, route, w_in, w_out):
    N, D = x.shape
    E = w_out.shape[0]
    tm = min(N, 512)
    gt = SUBLANES * LANES
    assert gt % tm == 0
    n_rows = -(-(TOP_K * N + E * tm) // gt) * gt
    src, dest, tile_exp, active = _moe_plan(route, E, tm, n_rows)
    xs = _moe_gather(xn, src, gt)
    ys = _moe_experts(xs, tile_exp, active, w_in, w_out, tm)
    return _moe_combine(x, route, ys, dest, min(N, gt))


def _decoder_group(x, shift0, wkv0, past, p):
    B, T, D = x.shape
    N = B * T
    r, lw, k, v, a, b, g, bonus, last = _rwkv_pre(x, shift0, p)
    y, s_fin = _rwkv_scan(r, lw, k, v, a, b, wkv0)
    x1 = _rwkv_post(y.reshape(N, D), bonus.reshape(N, D), g.reshape(N, D), x.reshape(N, D), p)
    x2 = _ffn(x1, p["norm_ffn0"], p["ffn_w_in"], p["ffn_w_out"])
    k_sh, v_sh, kb, vb, qb = _kvq(x2, p)
    sh3 = lambda z: z.reshape(B, T, D)
    if past is None:
        o = _sb_prompt(sh3(qb), sh3(kb), sh3(vb), p["sb_bias"])
    else:
        o = _sb_paged(sh3(qb), sh3(k_sh), sh3(v_sh), p["sb_bias"], *past)
    x3, xn3, route = _attn_out_router(o.reshape(N, D), x2, p)
    x4 = _moe(x3, xn3, route, p["moe_w_in"], p["moe_w_out"])
    H = D // HEAD_DIM
    return (x4.reshape(B, T, D), last.reshape(1, B, D), s_fin[None], k_sh.reshape(B, T, H, HEAD_DIM),
            v_sh.reshape(B, T, H, HEAD_DIM))


def kernel(x_prompt, x_sample, cache_k, cache_v, state_wkv, state_shift, page_table, norm_mix, norm_ffn, rw_mix, rw_w_r, rw_w_k, rw_w_v, rw_w_o, rw_w0, rw_w1, rw_w2, rw_a0, rw_a1, rw_a2, rw_g1, rw_g2, rw_k_k, rw_k_a, rw_r_k, rw_ln_w, rw_ln_b, kv_norm, kv_w, kv_k_norm, sb_w_q, sb_q_norm, sb_bias, sb_w_o, ffn_w_in, ffn_w_out, moe_router, moe_w_in, moe_w_out):
    assert norm_mix.shape[0] == 2 and state_wkv.shape[0] == 1, "one RWKV layer followed by one attention layer"
    p = dict(norm_mix0=norm_mix[0], norm_mix1=norm_mix[1], norm_ffn0=norm_ffn[0], norm_ffn1=norm_ffn[1],
             rw_mix=rw_mix[0], rw_w_r=rw_w_r[0], rw_w_k=rw_w_k[0], rw_w_v=rw_w_v[0], rw_w_o=rw_w_o[0],
             rw_w0=rw_w0[0], rw_w1=rw_w1[0], rw_w2=rw_w2[0], rw_a0=rw_a0[0], rw_a1=rw_a1[0], rw_a2=rw_a2[0],
             rw_g1=rw_g1[0], rw_g2=rw_g2[0], rw_k_k=rw_k_k[0], rw_k_a=rw_k_a[0], rw_r_k=rw_r_k[0].reshape(-1),
             rw_ln_w=rw_ln_w[0], rw_ln_b=rw_ln_b[0], kv_norm=kv_norm, kv_w=kv_w, kv_k_norm=kv_k_norm,
             sb_w_q=sb_w_q[0], sb_q_norm=sb_q_norm[0], sb_bias=sb_bias[0], sb_w_o=sb_w_o[0],
             ffn_w_in=ffn_w_in[0], ffn_w_out=ffn_w_out[0], moe_router=moe_router[0], moe_w_in=moe_w_in[0],
             moe_w_out=moe_w_out[0])
    bp, _, d = x_prompt.shape
    h = d // HEAD_DIM
    y_p, shift_p, wkv_p, k_p, v_p = _decoder_group(
        x_prompt, jnp.zeros((bp, d), x_prompt.dtype), jnp.zeros((bp, h, HEAD_DIM, HEAD_DIM), x_prompt.dtype), None, p)
    y_s, shift_s, wkv_s, k_s, v_s = _decoder_group(
        x_sample, state_shift[0], state_wkv[0], (cache_k, cache_v, page_table), p)
    return (y_p, y_s, wkv_p, shift_p, k_p, v_p, wkv_s, shift_s, k_s, v_s)
```

```python
import functools
import math

import jax
import jax.numpy as jnp
from jax import lax
from jax.experimental import pallas as pl
from jax.experimental.pallas import tpu as pltpu

F32 = jnp.float32
BF16 = jnp.bfloat16

HEAD_DIM = 64
LANES = 128
SUBLANES = 8
RMS_EPS = 1e-6
RW_GN_EPS = 64e-5
TOP_K = 2
ROUTE_W = 8
LOG2E = math.log2(math.e)
SCAN_CHUNK = 64
VMEM_LIMIT = 52 * 1024 * 1024


def _cparams(sem):
    return pltpu.CompilerParams(dimension_semantics=sem, vmem_limit_bytes=VMEM_LIMIT)


def _dot(a, b):
    return jnp.dot(a, b, preferred_element_type=F32)


def _dot_nt(a, b):
    return lax.dot_general(a, b, (((1,), (1,)), ((), ())), preferred_element_type=F32)


def _dot_tn(a, b):
    return lax.dot_general(a, b, (((0,), (0,)), ((), ())), preferred_element_type=F32)


def _split(x, n):
    parts = []
    rem = x
    for i in range(n):
        p = rem.astype(BF16)
        parts.append(p)
        if i + 1 < n:
            rem = rem - p.astype(F32)
    return parts


def _mm(a, b, kind="nn", na=3, nb=3):
    f = {"nn": _dot, "nt": _dot_nt, "tn": _dot_tn}[kind]
    ap = _split(a, na) if a.dtype != BF16 else [a]
    bp = _split(b, nb) if b.dtype != BF16 else [b]
    keep = max(len(ap), len(bp))
    out = None
    for i, x in enumerate(ap):
        for j, y in enumerate(bp):
            if i + j < keep:
                t = f(x, y)
                out = t if out is None else out + t
    return out


def _rms(x, gain):
    return x * lax.rsqrt(jnp.mean(x * x, axis=-1, keepdims=True) + RMS_EPS) * gain


def _softplus(z):
    return jnp.maximum(z, 0.0) + jnp.log(1.0 + jnp.exp(-jnp.abs(z)))


def _softplus2(z):
    return jnp.maximum(z, 0.0) + jnp.log(1.0 + jnp.exp2(-jnp.abs(z))) * LOG2E


def _sigmoid(z):
    return 1.0 / (1.0 + jnp.exp(-z))


def _head_sum(x, e, et):
    s = _mm(x, e, na=3)
    return _mm(s, et, na=3)


def _head_onehot(d):
    h = d // HEAD_DIM
    e = (jnp.arange(d)[:, None] // HEAD_DIM == jnp.arange(LANES)[None, :]).astype(BF16)
    del h
    return e, e.T


def _rwkv_pre_kernel(x_ref, xp_ref, sh_ref, nm_ref, mix_ref, wr_ref, wk_ref, wv_ref, w0_ref, w1_ref,
                     w2_ref, a0_ref, a1_ref, a2_ref, g1_ref, g2_ref, kk_ref, ka_ref, rk_ref, e_ref,
                     et_ref, r_o, lw_o, k_o, v_o, a_o, b_o, g_o, bonus_o, last_o):
    t = pl.program_id(1)
    gain = nm_ref[...]
    x = x_ref[0]
    tt = x.shape[0]
    xn = _rms(x, gain)
    prev_tile_last = _rms(xp_ref[0][SUBLANES - 1:SUBLANES, :], gain)
    first = jnp.where(t == 0, sh_ref[0], prev_tile_last)
    row = lax.broadcasted_iota(jnp.int32, xn.shape, 0)
    prev = jnp.where(row == 0, first, pltpu.roll(xn, 1, 0))
    dx = prev - xn

    def mixed(i):
        return (xn + dx * mix_ref[i:i + 1, :]).astype(BF16)

    r = _dot(mixed(0), wr_ref[...])
    lw_raw = w0_ref[...] + _dot(jnp.tanh(_dot(mixed(1), w1_ref[...])).astype(BF16), w2_ref[...])
    k = _dot(mixed(2), wk_ref[...])
    v = _dot(mixed(3), wv_ref[...])
    asig = _sigmoid(a0_ref[...] + _dot(_dot(mixed(4), a1_ref[...]).astype(BF16), a2_ref[...]))
    g = _dot(_sigmoid(_dot(mixed(5), g1_ref[...])).astype(BF16), g2_ref[...])

    log_w = -_softplus(-lw_raw) - 0.5
    lw_o[0] = -jnp.exp(log_w)
    e, et = e_ref[...], et_ref[...]
    kkr = k * kk_ref[...]
    nrm = jnp.maximum(jnp.sqrt(_head_sum(kkr * kkr, e, et)), 1e-12)
    kk = kkr * (1.0 / nrm)
    kmod = k * (1.0 + (asig - 1.0) * ka_ref[...])
    r_o[0] = r
    k_o[0] = kmod
    v_o[0] = v
    a_o[0] = -kk
    b_o[0] = kk * asig
    g_o[0] = g
    bonus_o[0] = _head_sum(r * kmod * rk_ref[...], e, et) * v
    last_o[0] = xn[tt - 1:tt, :]


def _rwkv_pre(x, shift0, p):
    B, T, D = x.shape
    tt = min(T, 256)
    nt = T // tt
    e, et = _head_onehot(D)
    row = lambda a: a.reshape(1, -1).astype(F32)
    bf = lambda a: a.astype(BF16)
    consts = [row(p["norm_mix0"]), p["rw_mix"].astype(F32), bf(p["rw_w_r"]), bf(p["rw_w_k"]), bf(p["rw_w_v"]),
              row(p["rw_w0"]), bf(p["rw_w1"]), bf(p["rw_w2"]), row(p["rw_a0"]), bf(p["rw_a1"]), bf(p["rw_a2"]),
              bf(p["rw_g1"]), bf(p["rw_g2"]), row(p["rw_k_k"]), row(p["rw_k_a"]), row(p["rw_r_k"]), e, et]
    const_specs = [pl.BlockSpec(c.shape, lambda b, t, n=c.ndim: (0,) * n) for c in consts]
    tile = pl.BlockSpec((1, tt, D), lambda b, t: (b, t, 0))
    prev8 = pl.BlockSpec((1, SUBLANES, D), lambda b, t: (b, jnp.maximum(t * (tt // SUBLANES) - 1, 0), 0))
    one = pl.BlockSpec((1, 1, D), lambda b, t: (b, 0, 0))
    big = jax.ShapeDtypeStruct((B, T, D), F32)
    outs = pl.pallas_call(
        _rwkv_pre_kernel,
        grid=(B, nt),
        in_specs=[tile, prev8, one] + const_specs,
        out_specs=[tile] * 8 + [one],
        out_shape=[big] * 8 + [jax.ShapeDtypeStruct((B, 1, D), F32)],
        compiler_params=_cparams(("parallel", "arbitrary")),
        name="rwkv_pre",
    )(x, x, shift0.reshape(B, 1, D), *consts)
    return outs


def _scan_kernel(r_ref, lw_ref, k_ref, v_ref, a_ref, b_ref, s0_ref, y_ref, sT_ref, s_scr, *, nchunk, npair):
    C = SCAN_CHUNK
    c = pl.program_id(2)

    @pl.when(c == 0)
    def _():
        s_scr[...] = s0_ref[0]

    row2 = lax.broadcasted_iota(jnp.int32, (2 * C, LANES), 0)
    lane2 = lax.broadcasted_iota(jnp.int32, (2 * C, LANES), 1)
    own = (lane2 // HEAD_DIM) == (row2 // C)
    ri = lax.broadcasted_iota(jnp.int32, (2 * C, 2 * C), 0)
    ci = lax.broadcasted_iota(jnp.int32, (2 * C, 2 * C), 1)
    strict = ci < ri
    incl = ci <= ri
    eye = (ci == ri).astype(F32)
    tri = (lax.broadcasted_iota(jnp.int32, (C, C), 1) <= lax.broadcasted_iota(jnp.int32, (C, C), 0)).astype(BF16)

    def stack(x):
        return jnp.where(own, jnp.concatenate([x, x], axis=0), 0.0).astype(BF16)

    def mm(x, y, kind="nn"):
        return _mm(x.astype(BF16), y.astype(BF16), kind)

    chains = [(p, g) for p in range(npair) for g in range(nchunk)]
    w = {}
    for ch in chains:
        p, g = ch
        sl, ln = pl.ds(g * C, C), pl.ds(p * LANES, LANES)
        r, lw, k, v, a, b = (ref[0, sl, ln] for ref in (r_ref, lw_ref, k_ref, v_ref, a_ref, b_ref))
        gc = _mm(tri, lw, nb=2)
        gt = gc[C - 1:C, :]
        eneg = jnp.exp(-gc)
        efar = jnp.exp(gt - gc)
        w[ch] = dict(rt=stack(r * jnp.exp(gc)), at=stack(a * jnp.exp(gc - lw)), bt=stack(b * eneg),
                     kt=stack(k * eneg), bw=stack(b * efar), kw=stack(k * efar), vs=stack(v), dec=jnp.exp(gt))
    for ch in chains:
        c_ = w[ch]
        G = mm(jnp.concatenate([c_["at"], c_["rt"]], axis=0), jnp.concatenate([c_["bt"], c_["kt"]], axis=0), "nt")
        c_["L"] = jnp.where(strict, G[:2 * C, :2 * C], 0.0)
        c_["Lak"] = jnp.where(strict, G[:2 * C, 2 * C:], 0.0)
        c_["RB"] = jnp.where(incl, G[2 * C:, :2 * C], 0.0).astype(BF16)
        c_["RK"] = jnp.where(incl, G[2 * C:, 2 * C:], 0.0)
    for ch in chains:
        c_ = w[ch]
        c_["Tinv"] = eye + c_["L"]
        c_["A"] = mm(c_["L"], c_["L"])
        c_["LV"] = mm(c_["Lak"], c_["vs"]).astype(BF16)
        c_["Y0"] = mm(c_["RK"], c_["vs"])
        c_["Z0"] = mm(c_["vs"], c_["kw"], "tn")
    for _ in range(int(math.log2(C)) - 2):
        for ch in chains:
            c_ = w[ch]
            both = mm(c_["A"], jnp.concatenate([c_["A"], c_["Tinv"]], axis=1))
            c_["A"] = both[:, :2 * C]
            c_["Tinv"] = c_["Tinv"] + both[:, 2 * C:]
    for ch in chains:
        c_ = w[ch]
        c_["Tinv"] = c_["Tinv"] + mm(c_["A"], c_["Tinv"])
    for ch in chains:
        c_ = w[ch]
        PQ = mm(c_["Tinv"], jnp.concatenate([c_["at"], c_["LV"]], axis=1))
        c_["P"] = PQ[:, :LANES].astype(BF16)
        c_["QT"] = PQ[:, LANES:].T
    S = [s_scr[p] for p in range(npair)]
    for g in range(nchunk):
        for p in range(npair):
            c_ = w[(p, g)]
            c_["S"] = S[p].astype(BF16)
            UT = mm(c_["S"], c_["P"], "nt") + c_["QT"]
            c_["UT"] = UT
            S[p] = S[p] * c_["dec"] + mm(UT, c_["bw"]) + c_["Z0"]
    for p in range(npair):
        s_scr[p] = S[p]
    for ch in chains:
        p, g = ch
        c_ = w[ch]
        Y = mm(c_["rt"], c_["S"], "nt") + mm(c_["RB"], c_["UT"].T) + c_["Y0"]
        y_ref[0, pl.ds(g * C, C), pl.ds(p * LANES, LANES)] = Y[:C] + Y[C:]

    @pl.when(c == pl.num_programs(2) - 1)
    def _():
        sT_ref[0] = s_scr[...]


def _rwkv_scan(r, lw, k, v, a, b, s0):
    B, T, D = r.shape
    H = D // HEAD_DIM
    NP = D // LANES
    C = SCAN_CHUNK
    Tp = -(-T // C) * C
    if Tp != T:
        pad = lambda z: jnp.pad(z, ((0, 0), (0, Tp - T), (0, 0)))
        r, lw, k, v, a, b = (pad(z) for z in (r, lw, k, v, a, b))
    nchunk = next(n for n in (8, 4, 1) if Tp % (n * C) == 0)
    npair = 2 if NP % 2 == 0 else 1
    tb = nchunk * C
    s0p = s0.astype(F32).reshape(B, NP, 2, HEAD_DIM, HEAD_DIM)
    z = jnp.zeros_like(s0p[:, :, 0])
    s0bd = jnp.concatenate([jnp.concatenate([s0p[:, :, 0], z], axis=-1),
                            jnp.concatenate([z, s0p[:, :, 1]], axis=-1)], axis=-2)
    tile = pl.BlockSpec((1, tb, npair * LANES), lambda bb, j, c: (bb, c, j))
    st = pl.BlockSpec((1, npair, LANES, LANES), lambda bb, j, c: (bb, j, 0, 0))
    y, sT = pl.pallas_call(
        functools.partial(_scan_kernel, nchunk=nchunk, npair=npair),
        grid=(B, NP // npair, Tp // tb),
        in_specs=[tile] * 6 + [st],
        out_specs=[tile, st],
        out_shape=[jax.ShapeDtypeStruct((B, Tp, D), F32), jax.ShapeDtypeStruct((B, NP, LANES, LANES), F32)],
        scratch_shapes=[pltpu.VMEM((npair, LANES, LANES), F32)],
        compiler_params=_cparams(("parallel", "parallel", "arbitrary")),
        name="rwkv_scan",
    )(r, lw, k, v, a, b, s0bd)
    s_fin = jnp.stack([sT[:, :, :HEAD_DIM, :HEAD_DIM], sT[:, :, HEAD_DIM:, HEAD_DIM:]], axis=2)
    return y[:, :T], s_fin.reshape(B, H, HEAD_DIM, HEAD_DIM)


def _rwkv_post_kernel(y_ref, bonus_ref, g_ref, x_ref, lnw_ref, lnb_ref, wo_ref, e_ref, et_ref, o_ref):
    e, et = e_ref[...], et_ref[...]
    y = y_ref[...]
    mu = _head_sum(y, e, et) * (1.0 / HEAD_DIM)
    d = y - mu
    var = _head_sum(d * d, e, et) * (1.0 / HEAD_DIM)
    yn = d * lax.rsqrt(var + RW_GN_EPS) * lnw_ref[...] + lnb_ref[...]
    z = ((yn + bonus_ref[...]) * g_ref[...]).astype(BF16)
    o_ref[...] = x_ref[...] + _dot(z, wo_ref[...])


def _rwkv_post(y, bonus, g, x, p):
    N, D = x.shape
    tm = min(N, 512)
    e, et = _head_onehot(D)
    consts = [p["rw_ln_w"].reshape(1, D).astype(F32), p["rw_ln_b"].reshape(1, D).astype(F32),
              p["rw_w_o"].astype(BF16), e, et]
    tile = pl.BlockSpec((tm, D), lambda i: (i, 0))
    return pl.pallas_call(
        _rwkv_post_kernel,
        grid=(N // tm,),
        in_specs=[tile] * 4 + [pl.BlockSpec(c.shape, lambda i: (0, 0)) for c in consts],
        out_specs=tile,
        out_shape=jax.ShapeDtypeStruct((N, D), F32),
        compiler_params=_cparams(("parallel",)),
        name="rwkv_post",
    )(y, bonus, g, x, *consts)


def _ffn_kernel(x_ref, nw_ref, wg_ref, wu_ref, wo_ref, o_ref, xn_scr, acc_scr):
    c = pl.program_id(1)

    @pl.when(c == 0)
    def _():
        xn_scr[...] = _rms(x_ref[...], nw_ref[...]).astype(BF16)
        acc_scr[...] = jnp.zeros_like(acc_scr)

    xn = xn_scr[...]
    gate = _dot(xn, wg_ref[...])
    up = _dot(xn, wu_ref[...])
    h = (gate * _sigmoid(gate) * up).astype(BF16)
    acc_scr[...] += _dot(h, wo_ref[...])

    @pl.when(c == pl.num_programs(1) - 1)
    def _():
        o_ref[...] = x_ref[...] + acc_scr[...]


def _ffn_chunk(f):
    for fc in (1408, 896, 512, 256, 128):
        if f % fc == 0:
            return fc
    return f


def _ffn(x, norm_w, w_in, w_out):
    N, D = x.shape
    F = w_out.shape[0]
    tm = min(N, 512)
    fc = _ffn_chunk(F)
    nfc = F // fc
    return pl.pallas_call(
        _ffn_kernel,
        grid=(N // tm, nfc),
        in_specs=[pl.BlockSpec((tm, D), lambda i, c: (i, 0)),
                  pl.BlockSpec((1, D), lambda i, c: (0, 0)),
                  pl.BlockSpec((D, fc), lambda i, c: (0, c)),
                  pl.BlockSpec((D, fc), lambda i, c: (0, nfc + c)),
                  pl.BlockSpec((fc, D), lambda i, c: (c, 0))],
        out_specs=pl.BlockSpec((tm, D), lambda i, c: (i, 0)),
        out_shape=jax.ShapeDtypeStruct((N, D), F32),
        scratch_shapes=[pltpu.VMEM((tm, D), BF16), pltpu.VMEM((tm, D), F32)],
        compiler_params=_cparams(("parallel", "arbitrary")),
        name="ffn",
    )(x, norm_w.reshape(1, D).astype(F32), w_in.astype(BF16), w_in.astype(BF16), w_out.astype(BF16))


def _kvq_kernel(x_ref, kvn_ref, qn_ref, wkv_ref, wq_ref, kg_ref, qg_ref, e_ref, et_ref,
                k_o, v_o, kb_o, vb_o, qb_o):
    e, et = e_ref[...], et_ref[...]
    x = x_ref[...]
    D = x.shape[1]
    xhat = x * lax.rsqrt(jnp.mean(x * x, axis=-1, keepdims=True) + RMS_EPS)
    kv = _dot((xhat * kvn_ref[...]).astype(BF16), wkv_ref[...])
    kraw = kv[:, :D]
    v = kv[:, D:]
    k = kraw * lax.rsqrt(_head_sum(kraw * kraw, e, et) * (1.0 / HEAD_DIM) + RMS_EPS) * kg_ref[...]
    q = _dot((xhat * qn_ref[...]).astype(BF16), wq_ref[...])
    q = q * lax.rsqrt(_head_sum(q * q, e, et) * (1.0 / HEAD_DIM) + RMS_EPS) * qg_ref[...]
    k_o[...] = k
    v_o[...] = v
    kb_o[...] = k.astype(BF16)
    vb_o[...] = v.astype(BF16)
    qb_o[...] = (q * (LOG2E / math.sqrt(HEAD_DIM))).astype(BF16)


def _kvq(x, p):
    N, D = x.shape
    H = D // HEAD_DIM
    tm = min(N, 512)
    e, et = _head_onehot(D)
    row = lambda a: a.reshape(1, D).astype(F32)
    consts = [row(p["kv_norm"]), row(p["norm_mix1"]), p["kv_w"].astype(BF16), p["sb_w_q"].astype(BF16),
              row(jnp.tile(p["kv_k_norm"], H)), row(jnp.tile(p["sb_q_norm"], H)), e, et]
    tile = pl.BlockSpec((tm, D), lambda i: (i, 0))
    f = jax.ShapeDtypeStruct((N, D), F32)
    h = jax.ShapeDtypeStruct((N, D), BF16)
    return pl.pallas_call(
        _kvq_kernel,
        grid=(N // tm,),
        in_specs=[tile] + [pl.BlockSpec(c.shape, lambda i: (0, 0)) for c in consts],
        out_specs=[tile] * 5,
        out_shape=[f, f, h, h, h],
        compiler_params=_cparams(("parallel",)),
        name="kvq_proj",
    )(x, *consts)


def _sb_prompt_kernel(bias_ref, q_ref, k_ref, v_ref, o_ref, acc_scr, car_scr, *, tq):
    j = pl.program_id(1)
    i = pl.program_id(2)
    q = q_ref[0]
    lane = lax.broadcasted_iota(jnp.int32, q.shape, 1)
    qh = [jnp.where(lane < HEAD_DIM, q, jnp.zeros_like(q)), jnp.where(lane >= HEAD_DIM, q, jnp.zeros_like(q))]
    ri = lax.broadcasted_iota(jnp.int32, (tq, tq), 0)
    ci = lax.broadcasted_iota(jnp.int32, (tq, tq), 1)
    later = (ri > ci).astype(BF16)
    causal = ci < ri
    rep = tq // LANES

    def kv(kb):
        start = pl.multiple_of(kb * tq, tq)
        return k_ref[0, pl.ds(start, tq), :], v_ref[0, pl.ds(start, tq), :]

    def logits(h, kblk):
        return _dot_nt(qh[h], kblk) + bias_ref[2 * j + h]

    def sums(z, masked):
        sp = _softplus2(z)
        if masked:
            sp = jnp.where(causal, sp, 0.0)
        tail = _dot(sp.astype(BF16), later)
        total = jnp.broadcast_to(tail[:, 0:1] + sp[:, 0:1], (tq, LANES))
        return z - sp - tail, total

    def scores(h, kblk, masked):
        return sums(logits(h, kblk), masked)

    def weights(part, carry, vblk, masked):
        pa = jnp.exp2(part if carry is None else part - jnp.tile(carry, (1, rep)))
        if masked:
            pa = jnp.where(causal, pa, 0.0)
        return _dot(pa.astype(BF16), vblk)

    kd, vd = kv(i)
    for h in range(2):
        part, total = scores(h, kd, True)
        acc_scr[h] = weights(part, None, vd, True)
        car_scr[h] = total

    def pair(s, _):
        kb = i - 1 - 2 * s
        ka, va = kv(kb)
        kc, vc = kv(kb - 1)
        zs = [(logits(h, ka), logits(h, kc)) for h in range(2)]
        ps = [(sums(za, False), sums(zc, False)) for za, zc in zs]
        for h in range(2):
            (part_a, tot_a), (part_c, tot_c) = ps[h]
            carry = car_scr[h]
            carry_c = carry + tot_a
            acc_scr[h] += weights(part_a, carry, va, False) + weights(part_c, carry_c, vc, False)
            car_scr[h] = carry_c + tot_c
        return 0

    lax.fori_loop(0, i // 2, pair, 0)

    @pl.when(i % 2 == 1)
    def _():
        k0, v0 = kv(0)
        for h in range(2):
            part, _ = scores(h, k0, False)
            acc_scr[h] += weights(part, car_scr[h], v0, False)

    o_ref[0] = jnp.where(lane < HEAD_DIM, acc_scr[0], acc_scr[1]).astype(o_ref.dtype)


def _sb_prompt(q, k, v, bias):
    B, T, D = q.shape
    NP = D // LANES
    tq = min(T, 256)
    grid_spec = pltpu.PrefetchScalarGridSpec(
        num_scalar_prefetch=1,
        grid=(B, NP, T // tq),
        in_specs=[pl.BlockSpec((1, tq, LANES), lambda b, j, i, bias: (b, i, j)),
                  pl.BlockSpec((1, T, LANES), lambda b, j, i, bias: (b, 0, j)),
                  pl.BlockSpec((1, T, LANES), lambda b, j, i, bias: (b, 0, j))],
        out_specs=pl.BlockSpec((1, tq, LANES), lambda b, j, i, bias: (b, i, j)),
        scratch_shapes=[pltpu.VMEM((2, tq, LANES), F32), pltpu.VMEM((2, tq, LANES), F32)],
    )
    return pl.pallas_call(
        functools.partial(_sb_prompt_kernel, tq=tq),
        grid_spec=grid_spec,
        out_shape=jax.ShapeDtypeStruct((B, T, D), BF16),
        compiler_params=_cparams(("parallel", "parallel", "arbitrary")),
        name="sb_attn_prompt",
    )(bias.astype(F32) * LOG2E, q, k, v)


def _sb_paged_kernel(pt_ref, q_ref, kn_ref, vn_ref, bias_ref, later_ref, *rest, tnew, page, heads, pps):
    k_refs, v_refs = rest[0:2 * pps:2], rest[1:2 * pps:2]
    o_ref, acc_scr, car_scr = rest[2 * pps:]
    s = pl.program_id(1)

    def segment(ks, vs, mask):
        nk = len(ks) * page
        q = q_ref[0]
        slab = 2 * tnew
        tiles = lambda refs, h: jnp.concatenate([r[0, h] for r in refs], axis=1).astype(BF16)
        zs = []
        for h in range(heads):
            qq = q[(h // 2) * slab:(h // 2 + 1) * slab, :]
            zz = _dot(qq, tiles(ks, h))
            zs.append(zz[(h % 2) * tnew:(h % 2 + 1) * tnew, :])
        z = jnp.concatenate(zs, axis=0) + bias_ref[:, :nk]
        sp = _softplus2(z)
        if mask is not None:
            sp = jnp.where(mask, sp, 0.0)
        tail = _dot(sp.astype(BF16), later_ref[:nk, :nk])
        carry = car_scr[...]
        pa = jnp.exp2(z - sp - tail - jnp.tile(carry, (1, nk // LANES)))
        if mask is not None:
            pa = jnp.where(mask, pa, 0.0)
        pa = pa.astype(BF16)
        for h in range(heads):
            res = _dot_nt(pa[(h // 2) * slab:(h // 2 + 1) * slab, :], tiles(vs, h))
            acc_scr[h * tnew:(h + 1) * tnew, :] += res[(h % 2) * tnew:(h % 2 + 1) * tnew, :]
        car_scr[...] = carry + jnp.broadcast_to(tail[:, 0:1] + sp[:, 0:1], carry.shape)

    @pl.when(s == 0)
    def _():
        acc_scr[...] = jnp.zeros_like(acc_scr)
        car_scr[...] = jnp.zeros_like(car_scr)
        key = lax.broadcasted_iota(jnp.int32, (LANES, page), 1)
        qi = lax.broadcasted_iota(jnp.int32, (LANES, page), 0) % tnew
        segment([kn_ref], [vn_ref], key < qi)

    @pl.when(s > 0)
    def _():
        segment(k_refs, v_refs, None)

    @pl.when(s == pl.num_programs(1) - 1)
    def _():
        o_ref[0] = acc_scr[...]


def _sb_paged(q, k_new, v_new, bias, cache_k, cache_v, page_table):
    B, tnew, D = q.shape
    H = D // HEAD_DIM
    n_pool, page = cache_k.shape[0], cache_k.shape[1]
    npg = page_table.shape[1]
    assert H * tnew == LANES and page == LANES and tnew % SUBLANES == 0
    ck = cache_k.transpose(0, 2, 3, 1)
    cv = cache_v.transpose(0, 2, 3, 1)
    new = lambda z: jnp.pad(z.reshape(B, tnew, H, HEAD_DIM).transpose(0, 2, 3, 1),
                            ((0, 0), (0, 0), (0, 0), (0, page - tnew)))
    qrows = q.reshape(B, tnew, H, HEAD_DIM).transpose(0, 2, 1, 3).reshape(B, LANES, HEAD_DIM)
    pps = next(n for n in (4, 2, 1) if npg % n == 0)
    nk = pps * page
    bias_rows = jnp.broadcast_to(jnp.repeat(bias.astype(F32) * LOG2E, tnew)[:, None], (LANES, nk))
    later = (jnp.arange(nk)[:, None] > jnp.arange(nk)[None, :]).astype(BF16)

    def page_map(slot):
        return lambda b, s, pt: (pt[b * npg + npg - jnp.maximum(s, 1) * pps + slot], 0, 0, 0)

    seq3 = lambda b, s, pt: (b, 0, 0)
    seq4 = lambda b, s, pt: (b, 0, 0, 0)
    const = lambda b, s, pt: (0, 0)
    pg = (1, H, HEAD_DIM, page)
    page_specs, page_args = [], []
    for slot in range(pps):
        page_specs += [pl.BlockSpec(pg, page_map(slot)), pl.BlockSpec(pg, page_map(slot))]
        page_args += [ck, cv]
    grid_spec = pltpu.PrefetchScalarGridSpec(
        num_scalar_prefetch=1,
        grid=(B, npg // pps + 1),
        in_specs=[pl.BlockSpec((1, LANES, HEAD_DIM), seq3),
                  pl.BlockSpec(pg, seq4), pl.BlockSpec(pg, seq4),
                  pl.BlockSpec((LANES, nk), const), pl.BlockSpec((nk, nk), const)] + page_specs,
        out_specs=pl.BlockSpec((1, LANES, HEAD_DIM), seq3),
        scratch_shapes=[pltpu.VMEM((LANES, HEAD_DIM), F32), pltpu.VMEM((LANES, LANES), F32)],
    )
    o = pl.pallas_call(
        functools.partial(_sb_paged_kernel, tnew=tnew, page=page, heads=H, pps=pps),
        grid_spec=grid_spec,
        out_shape=jax.ShapeDtypeStruct((B, LANES, HEAD_DIM), F32),
        compiler_params=_cparams(("parallel", "arbitrary")),
        name="sb_attn_paged",
    )(page_table.reshape(-1).astype(jnp.int32), qrows, new(k_new), new(v_new), bias_rows, later, *page_args)
    return o.reshape(B, H, tnew, HEAD_DIM).transpose(0, 2, 1, 3).reshape(B, tnew, D).astype(BF16)


def _attn_out_router_kernel(o_ref, x_ref, wo_ref, nw_ref, rt_ref, x_o, xn_o, route_o):
    x = x_ref[...] + _dot(o_ref[...], wo_ref[...])
    xn = _rms(x, nw_ref[...])
    x_o[...] = x
    xn_o[...] = xn
    n_exp = rt_ref.shape[1]
    logits = _mm(xn, rt_ref[...])
    lane = lax.broadcasted_iota(jnp.int32, logits.shape, 1).astype(F32)
    v1 = jnp.max(logits, axis=-1, keepdims=True)
    i1 = jnp.min(jnp.where(logits == v1, lane, float(n_exp)), axis=-1, keepdims=True)
    rest = jnp.where(lane == i1, -jnp.inf, logits)
    v2 = jnp.max(rest, axis=-1, keepdims=True)
    i2 = jnp.min(jnp.where(rest == v2, lane, float(n_exp)), axis=-1, keepdims=True)
    e2 = jnp.exp(v2 - v1)
    g1 = 1.0 / (1.0 + e2)
    g2 = e2 * g1
    col = lax.broadcasted_iota(jnp.int32, (x.shape[0], ROUTE_W), 1)
    route_o[...] = (jnp.where(col == 0, i1, 0.0) + jnp.where(col == 1, i2, 0.0)
                    + jnp.where(col == 2, g1, 0.0) + jnp.where(col == 3, g2, 0.0))


def _attn_out_router(o, x, p):
    N, D = x.shape
    tm = min(N, 512)
    E = p["moe_router"].shape[1]
    router = p["moe_router"].astype(F32)
    tile = pl.BlockSpec((tm, D), lambda i: (i, 0))
    return pl.pallas_call(
        _attn_out_router_kernel,
        grid=(N // tm,),
        in_specs=[tile, tile, pl.BlockSpec((D, D), lambda i: (0, 0)), pl.BlockSpec((1, D), lambda i: (0, 0)),
                  pl.BlockSpec((D, E), lambda i: (0, 0))],
        out_specs=[tile, tile, pl.BlockSpec((tm, ROUTE_W), lambda i: (i, 0))],
        out_shape=[jax.ShapeDtypeStruct((N, D), F32), jax.ShapeDtypeStruct((N, D), F32),
                   jax.ShapeDtypeStruct((N, ROUTE_W), F32)],
        compiler_params=_cparams(("parallel",)),
        name="attn_out_router",
    )(o, x, p["sb_w_o"].astype(BF16), p["norm_ffn1"].reshape(1, D).astype(F32), router)


def _moe_plan(route, n_exp, tm, n_rows):
    n = route.shape[0]
    eid = jnp.concatenate([route[:, 0], route[:, 1]]).astype(jnp.int32)
    onehot = (eid[:, None] == jnp.arange(n_exp, dtype=jnp.int32)[None, :]).astype(jnp.int32)
    cum = jnp.cumsum(onehot, axis=0)
    counts = cum[-1]
    rank = jnp.sum(onehot * (cum - 1), axis=1)
    padded = (counts + tm - 1) // tm * tm
    ends = jnp.cumsum(padded)
    dest = (ends - padded)[eid] + rank
    src = jnp.zeros((n_rows,), jnp.int32).at[dest].set(jnp.tile(jnp.arange(n, dtype=jnp.int32), TOP_K))
    tile_start = jnp.arange(n_rows // tm, dtype=jnp.int32) * tm
    tile_exp = jnp.minimum(jnp.searchsorted(ends, tile_start, side="right"), n_exp - 1).astype(jnp.int32)
    active = (tile_start < ends[-1]).astype(jnp.int32)
    return src, dest, tile_exp, active


def _row_copy(src_hbm, row, dst, slot, sem):
    return pltpu.make_async_copy(src_hbm.at[pl.ds(row, 1), :], dst.at[pl.ds(slot, 1), :], sem)


def _moe_gather_kernel(src_ref, x_hbm, o_hbm, sem, *, rows):
    base = pl.program_id(0) * rows

    def issue(r, carry):
        _row_copy(x_hbm, src_ref[r // LANES, r % LANES], o_hbm, base + r, sem).start()
        return carry

    lax.fori_loop(0, rows, issue, 0, unroll=8)
    pltpu.make_async_copy(x_hbm.at[pl.ds(0, rows), :], o_hbm.at[pl.ds(base, rows), :], sem).wait()


def _moe_gather(xn, src, gt):
    n_rows = src.shape[0]
    D = xn.shape[1]
    return pl.pallas_call(
        functools.partial(_moe_gather_kernel, rows=gt),
        grid=(n_rows // gt,),
        in_specs=[pl.BlockSpec((gt // LANES, LANES), lambda i: (i, 0), memory_space=pltpu.SMEM),
                  pl.BlockSpec(memory_space=pl.ANY)],
        out_specs=pl.BlockSpec(memory_space=pl.ANY),
        out_shape=jax.ShapeDtypeStruct((n_rows, D), xn.dtype),
        scratch_shapes=[pltpu.SemaphoreType.DMA(())],
        compiler_params=_cparams(("arbitrary",)),
        name="moe_gather",
    )(src.reshape(n_rows // LANES, LANES), xn)


def _moe_experts_kernel(te_ref, act_ref, xs_ref, wg_ref, wu_ref, wo_ref, o_ref, acc_scr):
    i = pl.program_id(0)
    c = pl.program_id(1)

    @pl.when(c == 0)
    def _():
        acc_scr[...] = jnp.zeros_like(acc_scr)

    @pl.when(act_ref[i] > 0)
    def _():
        xs = xs_ref[...].astype(BF16)
        gate = _dot(xs, wg_ref[0])
        up = _dot(xs, wu_ref[0])
        h = (gate * _sigmoid(gate) * up).astype(BF16)
        acc_scr[...] += _dot(h, wo_ref[0])

    @pl.when(c == pl.num_programs(1) - 1)
    def _():
        o_ref[...] = acc_scr[...]


def _moe_experts(xs, tile_exp, active, w_in, w_out, tm):
    n_rows, D = xs.shape
    F = w_out.shape[1]
    fc = _ffn_chunk(F)
    nfc = F // fc
    wi = w_in.astype(BF16)
    grid_spec = pltpu.PrefetchScalarGridSpec(
        num_scalar_prefetch=2,
        grid=(n_rows // tm, nfc),
        in_specs=[pl.BlockSpec((tm, D), lambda i, c, te, act: (i, 0)),
                  pl.BlockSpec((1, D, fc), lambda i, c, te, act: (te[i], 0, c * act[i])),
                  pl.BlockSpec((1, D, fc), lambda i, c, te, act: (te[i], 0, nfc + c * act[i])),
                  pl.BlockSpec((1, fc, D), lambda i, c, te, act: (te[i], c * act[i], 0))],
        out_specs=pl.BlockSpec((tm, D), lambda i, c, te, act: (i, 0)),
        scratch_shapes=[pltpu.VMEM((tm, D), F32)],
    )
    return pl.pallas_call(
        _moe_experts_kernel,
        grid_spec=grid_spec,
        out_shape=jax.ShapeDtypeStruct((n_rows, D), F32),
        compiler_params=_cparams(("parallel", "arbitrary")),
        name="moe_experts",
    )(tile_exp, active, xs, wi, wi, w_out.astype(BF16))


def _moe_combine_kernel(pos_ref, x_ref, route_ref, ys_hbm, o_ref, buf, sem):
    tm = x_ref.shape[0]

    def issue(t, carry):
        for k in range(TOP_K):
            _row_copy(ys_hbm, pos_ref[k, t // LANES, t % LANES], buf.at[k], t, sem).start()
        return carry

    lax.fori_loop(0, tm, issue, 0, unroll=4)
    for k in range(TOP_K):
        pltpu.make_async_copy(ys_hbm.at[pl.ds(0, tm), :], buf.at[k], sem).wait()
    route = route_ref[...]
    o_ref[...] = x_ref[...] + route[:, 2:3] * buf[0] + route[:, 3:4] * buf[1]


def _moe_combine(x, route, ys, dest, ct):
    N, D = x.shape
    pos = dest.reshape(TOP_K, N // LANES, LANES)
    return pl.pallas_call(
        _moe_combine_kernel,
        grid=(N // ct,),
        in_specs=[pl.BlockSpec((TOP_K, ct // LANES, LANES), lambda i: (0, i, 0), memory_space=pltpu.SMEM),
                  pl.BlockSpec((ct, D), lambda i: (i, 0)),
                  pl.BlockSpec((ct, ROUTE_W), lambda i: (i, 0)),
                  pl.BlockSpec(memory_space=pl.ANY)],
        out_specs=pl.BlockSpec((ct, D), lambda i: (i, 0)),
        out_shape=jax.ShapeDtypeStruct((N, D), F32),
        scratch_shapes=[pltpu.VMEM((TOP_K, ct, D), F32), pltpu.SemaphoreType.DMA(())],
        compiler_params=_cparams(("arbitrary",)),
        name="moe_combine",
    )(pos, x, route, ys)


def _moe(x, xn, route, w_in, w_out):
    N, D = x.shape
    E = w_out.shape[0]
    tm = min(N, 512)
    gt = SUBLANES * LANES
    assert gt % tm == 0
    n_rows = -(-(TOP_K * N + E * tm) // gt) * gt
    src, dest, tile_exp, active = _moe_plan(route, E, tm, n_rows)
    xs = _moe_gather(xn, src, gt)
    ys = _moe_experts(xs, tile_exp, active, w_in, w_out, tm)
    return _moe_combine(x, route, ys, dest, min(N, gt))


def _decoder_group(x, shift0, wkv0, past, p):
    B, T, D = x.shape
    N = B * T
    r, lw, k, v, a, b, g, bonus, last = _rwkv_pre(x, shift0, p)
    y, s_fin = _rwkv_scan(r, lw, k, v, a, b, wkv0)
    x1 = _rwkv_post(y.reshape(N, D), bonus.reshape(N, D), g.reshape(N, D), x.reshape(N, D), p)
    x2 = _ffn(x1, p["norm_ffn0"], p["ffn_w_in"], p["ffn_w_out"])
    k_sh, v_sh, kb, vb, qb = _kvq(x2, p)
    sh3 = lambda z: z.reshape(B, T, D)
    if past is None:
        o = _sb_prompt(sh3(qb), sh3(kb), sh3(vb), p["sb_bias"])
    else:
        o = _sb_paged(sh3(qb), sh3(k_sh), sh3(v_sh), p["sb_bias"], *past)
    x3, xn3, route = _attn_out_router(o.reshape(N, D), x2, p)
    x4 = _moe(x3, xn3, route, p["moe_w_in"], p["moe_w_out"])
    H = D // HEAD_DIM
    return (x4.reshape(B, T, D), last.reshape(1, B, D), s_fin[None], k_sh.reshape(B, T, H, HEAD_DIM),
            v_sh.reshape(B, T, H, HEAD_DIM))


def kernel(x_prompt, x_sample, cache_k, cache_v, state_wkv, state_shift, page_table, norm_mix, norm_ffn, rw_mix, rw_w_r, rw_w_k, rw_w_v, rw_w_o, rw_w0, rw_w1, rw_w2, rw_a0, rw_a1, rw_a2, rw_g1, rw_g2, rw_k_k, rw_k_a, rw_r_k, rw_ln_w, rw_ln_b, kv_norm, kv_w, kv_k_norm, sb_w_q, sb_q_norm, sb_bias, sb_w_o, ffn_w_in, ffn_w_out, moe_router, moe_w_in, moe_w_out):
    assert norm_mix.shape[0] == 2 and state_wkv.shape[0] == 1, "one RWKV layer followed by one attention layer"
    p = dict(norm_mix0=norm_mix[0], norm_mix1=norm_mix[1], norm_ffn0=norm_ffn[0], norm_ffn1=norm_ffn[1],
             rw_mix=rw_mix[0], rw_w_r=rw_w_r[0], rw_w_k=rw_w_k[0], rw_w_v=rw_w_v[0], rw_w_o=rw_w_o[0],
             rw_w0=rw_w0[0], rw_w1=rw_w1[0], rw_w2=rw_w2[0], rw_a0=rw_a0[0], rw_a1=rw_a1[0], rw_a2=rw_a2[0],
             rw_g1=rw_g1[0], rw_g2=rw_g2[0], rw_k_k=rw_k_k[0], rw_k_a=rw_k_a[0], rw_r_k=rw_r_k[0].reshape(-1),
             rw_ln_w=rw_ln_w[0], rw_ln_b=rw_ln_b[0], kv_norm=kv_norm, kv_w=kv_w, kv_k_norm=kv_k_norm,
             sb_w_q=sb_w_q[0], sb_q_norm=sb_q_norm[0], sb_bias=sb_bias[0], sb_w_o=sb_w_o[0],
             ffn_w_in=ffn_w_in[0], ffn_w_out=ffn_w_out[0], moe_router=moe_router[0], moe_w_in=moe_w_in[0],
             moe_w_out=moe_w_out[0])
    bp, _, d = x_prompt.shape
    h = d // HEAD_DIM
    y_p, shift_p, wkv_p, k_p, v_p = _decoder_group(
        x_prompt, jnp.zeros((bp, d), x_prompt.dtype), jnp.zeros((bp, h, HEAD_DIM, HEAD_DIM), x_prompt.dtype), None, p)
    y_s, shift_s, wkv_s, k_s, v_s = _decoder_group(
        x_sample, state_shift[0], state_wkv[0], (cache_k, cache_v, page_table), p)
    return (y_p, y_s, wkv_p, shift_p, k_p, v_p, wkv_s, shift_s, k_s, v_s)
```

```python
import functools
import math

import jax
import jax.numpy as jnp
from jax import lax
from jax.experimental import pallas as pl
from jax.experimental.pallas import tpu as pltpu

F32 = jnp.float32
BF16 = jnp.bfloat16

HEAD_DIM = 64
LANES = 128
SUBLANES = 8
RMS_EPS = 1e-6
RW_GN_EPS = 64e-5
TOP_K = 2
ROUTE_W = 8
LOG2E = math.log2(math.e)
WALK = 4
SCAN_CHUNK = 64
VMEM_LIMIT = 52 * 1024 * 1024


def _cparams(sem):
    return pltpu.CompilerParams(dimension_semantics=sem, vmem_limit_bytes=VMEM_LIMIT)


def _dot(a, b):
    return jnp.dot(a, b, preferred_element_type=F32)


def _dot_nt(a, b):
    return lax.dot_general(a, b, (((1,), (1,)), ((), ())), preferred_element_type=F32)


def _dot_tn(a, b):
    return lax.dot_general(a, b, (((0,), (0,)), ((), ())), preferred_element_type=F32)


def _split(x, n):
    parts = []
    rem = x
    for i in range(n):
        p = rem.astype(BF16)
        parts.append(p)
        if i + 1 < n:
            rem = rem - p.astype(F32)
    return parts


def _mm(a, b, kind="nn", na=3, nb=3):
    f = {"nn": _dot, "nt": _dot_nt, "tn": _dot_tn}[kind]
    ap = _split(a, na) if a.dtype != BF16 else [a]
    bp = _split(b, nb) if b.dtype != BF16 else [b]
    keep = max(len(ap), len(bp))
    out = None
    for i, x in enumerate(ap):
        for j, y in enumerate(bp):
            if i + j < keep:
                t = f(x, y)
                out = t if out is None else out + t
    return out


def _rms(x, gain):
    return x * lax.rsqrt(jnp.mean(x * x, axis=-1, keepdims=True) + RMS_EPS) * gain


def _softplus(z):
    return jnp.maximum(z, 0.0) + jnp.log(1.0 + jnp.exp(-jnp.abs(z)))


def _softplus2(z):
    neg_abs = lax.bitcast_convert_type(lax.bitcast_convert_type(z, jnp.uint32) | jnp.uint32(0x80000000), F32)
    return jnp.maximum(z, 0.0) + jnp.log(1.0 + jnp.exp2(neg_abs)) * LOG2E


def _sigmoid(z):
    return 1.0 / (1.0 + jnp.exp(-z))


def _head_sum(x, e, et):
    s = _mm(x, e, na=3)
    return _mm(s, et, na=3)


def _head_onehot(d):
    h = d // HEAD_DIM
    e = (jnp.arange(d)[:, None] // HEAD_DIM == jnp.arange(LANES)[None, :]).astype(BF16)
    del h
    return e, e.T


def _rwkv_pre_kernel(x_ref, xp_ref, sh_ref, nm_ref, mix_ref, wr_ref, wk_ref, wv_ref, w0_ref, w1_ref,
                     w2_ref, a0_ref, a1_ref, a2_ref, g1_ref, g2_ref, kk_ref, ka_ref, rk_ref, e_ref,
                     et_ref, r_o, lw_o, k_o, v_o, a_o, b_o, g_o, bonus_o, last_o):
    t = pl.program_id(1)
    gain = nm_ref[...]
    x = x_ref[0]
    tt = x.shape[0]
    xn = _rms(x, gain)
    prev_tile_last = _rms(xp_ref[0][SUBLANES - 1:SUBLANES, :], gain)
    first = jnp.where(t == 0, sh_ref[0], prev_tile_last)
    row = lax.broadcasted_iota(jnp.int32, xn.shape, 0)
    prev = jnp.where(row == 0, first, pltpu.roll(xn, 1, 0))
    dx = prev - xn

    def mixed(i):
        return (xn + dx * mix_ref[i:i + 1, :]).astype(BF16)

    r = _dot(mixed(0), wr_ref[...])
    lw_raw = w0_ref[...] + _dot(jnp.tanh(_dot(mixed(1), w1_ref[...])).astype(BF16), w2_ref[...])
    k = _dot(mixed(2), wk_ref[...])
    v = _dot(mixed(3), wv_ref[...])
    asig = _sigmoid(a0_ref[...] + _dot(_dot(mixed(4), a1_ref[...]).astype(BF16), a2_ref[...]))
    g = _dot(_sigmoid(_dot(mixed(5), g1_ref[...])).astype(BF16), g2_ref[...])

    log_w = -_softplus(-lw_raw) - 0.5
    lw_o[0] = -jnp.exp(log_w)
    e, et = e_ref[...], et_ref[...]
    kkr = k * kk_ref[...]
    nrm = jnp.maximum(jnp.sqrt(_head_sum(kkr * kkr, e, et)), 1e-12)
    kk = kkr * (1.0 / nrm)
    kmod = k * (1.0 + (asig - 1.0) * ka_ref[...])
    r_o[0] = r
    k_o[0] = kmod
    v_o[0] = v
    a_o[0] = -kk
    b_o[0] = kk * asig
    g_o[0] = g
    bonus_o[0] = _head_sum(r * kmod * rk_ref[...], e, et) * v
    last_o[0] = xn[tt - 1:tt, :]


def _rwkv_pre(x, shift0, p):
    B, T, D = x.shape
    tt = min(T, 256)
    nt = T // tt
    e, et = _head_onehot(D)
    row = lambda a: a.reshape(1, -1).astype(F32)
    bf = lambda a: a.astype(BF16)
    consts = [row(p["norm_mix0"]), p["rw_mix"].astype(F32), bf(p["rw_w_r"]), bf(p["rw_w_k"]), bf(p["rw_w_v"]),
              row(p["rw_w0"]), bf(p["rw_w1"]), bf(p["rw_w2"]), row(p["rw_a0"]), bf(p["rw_a1"]), bf(p["rw_a2"]),
              bf(p["rw_g1"]), bf(p["rw_g2"]), row(p["rw_k_k"]), row(p["rw_k_a"]), row(p["rw_r_k"]), e, et]
    const_specs = [pl.BlockSpec(c.shape, lambda b, t, n=c.ndim: (0,) * n) for c in consts]
    tile = pl.BlockSpec((1, tt, D), lambda b, t: (b, t, 0))
    prev8 = pl.BlockSpec((1, SUBLANES, D), lambda b, t: (b, jnp.maximum(t * (tt // SUBLANES) - 1, 0), 0))
    one = pl.BlockSpec((1, 1, D), lambda b, t: (b, 0, 0))
    big = jax.ShapeDtypeStruct((B, T, D), F32)
    outs = pl.pallas_call(
        _rwkv_pre_kernel,
        grid=(B, nt),
        in_specs=[tile, prev8, one] + const_specs,
        out_specs=[tile] * 8 + [one],
        out_shape=[big] * 8 + [jax.ShapeDtypeStruct((B, 1, D), F32)],
        compiler_params=_cparams(("parallel", "arbitrary")),
        name="rwkv_pre",
    )(x, x, shift0.reshape(B, 1, D), *consts)
    return outs


def _scan_kernel(r_ref, lw_ref, k_ref, v_ref, a_ref, b_ref, s0_ref, y_ref, sT_ref, s_scr, *, nchunk, npair):
    C = SCAN_CHUNK
    c = pl.program_id(2)

    @pl.when(c == 0)
    def _():
        s_scr[...] = s0_ref[0]

    row2 = lax.broadcasted_iota(jnp.int32, (2 * C, LANES), 0)
    lane2 = lax.broadcasted_iota(jnp.int32, (2 * C, LANES), 1)
    own = (lane2 // HEAD_DIM) == (row2 // C)
    ri = lax.broadcasted_iota(jnp.int32, (2 * C, 2 * C), 0)
    ci = lax.broadcasted_iota(jnp.int32, (2 * C, 2 * C), 1)
    strict = ci < ri
    incl = ci <= ri
    eye = (ci == ri).astype(F32)
    tri = (lax.broadcasted_iota(jnp.int32, (C, C), 1) <= lax.broadcasted_iota(jnp.int32, (C, C), 0)).astype(BF16)

    def stack(x):
        return jnp.where(own, jnp.concatenate([x, x], axis=0), 0.0).astype(BF16)

    def mm(x, y, kind="nn"):
        return _mm(x.astype(BF16), y.astype(BF16), kind)

    chains = [(p, g) for p in range(npair) for g in range(nchunk)]
    w = {}
    for ch in chains:
        p, g = ch
        sl, ln = pl.ds(g * C, C), pl.ds(p * LANES, LANES)
        r, lw, k, v, a, b = (ref[0, sl, ln] for ref in (r_ref, lw_ref, k_ref, v_ref, a_ref, b_ref))
        gc = _mm(tri, lw, nb=2)
        gt = gc[C - 1:C, :]
        eneg = jnp.exp(-gc)
        efar = jnp.exp(gt - gc)
        w[ch] = dict(rt=stack(r * jnp.exp(gc)), at=stack(a * jnp.exp(gc - lw)), bt=stack(b * eneg),
                     kt=stack(k * eneg), bw=stack(b * efar), kw=stack(k * efar), vs=stack(v), dec=jnp.exp(gt))
    for ch in chains:
        c_ = w[ch]
        G = mm(jnp.concatenate([c_["at"], c_["rt"]], axis=0), jnp.concatenate([c_["bt"], c_["kt"]], axis=0), "nt")
        c_["L"] = jnp.where(strict, G[:2 * C, :2 * C], 0.0)
        c_["Lak"] = jnp.where(strict, G[:2 * C, 2 * C:], 0.0)
        c_["RB"] = jnp.where(incl, G[2 * C:, :2 * C], 0.0).astype(BF16)
        c_["RK"] = jnp.where(incl, G[2 * C:, 2 * C:], 0.0)
    for ch in chains:
        c_ = w[ch]
        c_["Tinv"] = eye + c_["L"]
        c_["A"] = mm(c_["L"], c_["L"])
        c_["LV"] = mm(c_["Lak"], c_["vs"]).astype(BF16)
        c_["Y0"] = mm(c_["RK"], c_["vs"])
        c_["Z0"] = mm(c_["vs"], c_["kw"], "tn")
    for _ in range(int(math.log2(C)) - 2):
        for ch in chains:
            c_ = w[ch]
            both = mm(c_["A"], jnp.concatenate([c_["A"], c_["Tinv"]], axis=1))
            c_["A"] = both[:, :2 * C]
            c_["Tinv"] = c_["Tinv"] + both[:, 2 * C:]
    for ch in chains:
        c_ = w[ch]
        c_["Tinv"] = c_["Tinv"] + mm(c_["A"], c_["Tinv"])
    for ch in chains:
        c_ = w[ch]
        PQ = mm(c_["Tinv"], jnp.concatenate([c_["at"], c_["LV"]], axis=1))
        c_["P"] = PQ[:, :LANES].astype(BF16)
        c_["QT"] = PQ[:, LANES:].T
    S = [s_scr[p] for p in range(npair)]
    for g in range(nchunk):
        for p in range(npair):
            c_ = w[(p, g)]
            c_["S"] = S[p].astype(BF16)
            UT = mm(c_["S"], c_["P"], "nt") + c_["QT"]
            c_["UT"] = UT
            S[p] = S[p] * c_["dec"] + mm(UT, c_["bw"]) + c_["Z0"]
    for p in range(npair):
        s_scr[p] = S[p]
    for ch in chains:
        p, g = ch
        c_ = w[ch]
        Y = mm(c_["rt"], c_["S"], "nt") + mm(c_["RB"], c_["UT"].T) + c_["Y0"]
        y_ref[0, pl.ds(g * C, C), pl.ds(p * LANES, LANES)] = Y[:C] + Y[C:]

    @pl.when(c == pl.num_programs(2) - 1)
    def _():
        sT_ref[0] = s_scr[...]


def _rwkv_scan(r, lw, k, v, a, b, s0):
    B, T, D = r.shape
    H = D // HEAD_DIM
    NP = D // LANES
    C = SCAN_CHUNK
    Tp = -(-T // C) * C
    if Tp != T:
        pad = lambda z: jnp.pad(z, ((0, 0), (0, Tp - T), (0, 0)))
        r, lw, k, v, a, b = (pad(z) for z in (r, lw, k, v, a, b))
    nchunk = next(n for n in (8, 4, 1) if Tp % (n * C) == 0)
    npair = next(n for n in (8, 4, 2, 1) if NP % n == 0 and n * nchunk <= 16)
    tb = nchunk * C
    s0p = s0.astype(F32).reshape(B, NP, 2, HEAD_DIM, HEAD_DIM)
    z = jnp.zeros_like(s0p[:, :, 0])
    s0bd = jnp.concatenate([jnp.concatenate([s0p[:, :, 0], z], axis=-1),
                            jnp.concatenate([z, s0p[:, :, 1]], axis=-1)], axis=-2)
    tile = pl.BlockSpec((1, tb, npair * LANES), lambda bb, j, c: (bb, c, j))
    st = pl.BlockSpec((1, npair, LANES, LANES), lambda bb, j, c: (bb, j, 0, 0))
    y, sT = pl.pallas_call(
        functools.partial(_scan_kernel, nchunk=nchunk, npair=npair),
        grid=(B, NP // npair, Tp // tb),
        in_specs=[tile] * 6 + [st],
        out_specs=[tile, st],
        out_shape=[jax.ShapeDtypeStruct((B, Tp, D), F32), jax.ShapeDtypeStruct((B, NP, LANES, LANES), F32)],
        scratch_shapes=[pltpu.VMEM((npair, LANES, LANES), F32)],
        compiler_params=_cparams(("parallel", "parallel", "arbitrary")),
        name="rwkv_scan",
    )(r, lw, k, v, a, b, s0bd)
    s_fin = jnp.stack([sT[:, :, :HEAD_DIM, :HEAD_DIM], sT[:, :, HEAD_DIM:, HEAD_DIM:]], axis=2)
    return y[:, :T], s_fin.reshape(B, H, HEAD_DIM, HEAD_DIM)


def _rwkv_post_kernel(y_ref, bonus_ref, g_ref, x_ref, lnw_ref, lnb_ref, wo_ref, e_ref, et_ref, o_ref):
    e, et = e_ref[...], et_ref[...]
    y = y_ref[...]
    mu = _head_sum(y, e, et) * (1.0 / HEAD_DIM)
    d = y - mu
    var = _head_sum(d * d, e, et) * (1.0 / HEAD_DIM)
    yn = d * lax.rsqrt(var + RW_GN_EPS) * lnw_ref[...] + lnb_ref[...]
    z = ((yn + bonus_ref[...]) * g_ref[...]).astype(BF16)
    o_ref[...] = x_ref[...] + _dot(z, wo_ref[...])


def _rwkv_post(y, bonus, g, x, p):
    N, D = x.shape
    tm = min(N, 512)
    e, et = _head_onehot(D)
    consts = [p["rw_ln_w"].reshape(1, D).astype(F32), p["rw_ln_b"].reshape(1, D).astype(F32),
              p["rw_w_o"].astype(BF16), e, et]
    tile = pl.BlockSpec((tm, D), lambda i: (i, 0))
    return pl.pallas_call(
        _rwkv_post_kernel,
        grid=(N // tm,),
        in_specs=[tile] * 4 + [pl.BlockSpec(c.shape, lambda i: (0, 0)) for c in consts],
        out_specs=tile,
        out_shape=jax.ShapeDtypeStruct((N, D), F32),
        compiler_params=_cparams(("parallel",)),
        name="rwkv_post",
    )(y, bonus, g, x, *consts)


def _ffn_kernel(x_ref, nw_ref, wg_ref, wu_ref, wo_ref, o_ref, xn_scr, acc_scr):
    c = pl.program_id(1)

    @pl.when(c == 0)
    def _():
        xn_scr[...] = _rms(x_ref[...], nw_ref[...]).astype(BF16)
        acc_scr[...] = jnp.zeros_like(acc_scr)

    xn = xn_scr[...]
    gate = _dot(xn, wg_ref[...])
    up = _dot(xn, wu_ref[...])
    h = (gate * _sigmoid(gate) * up).astype(BF16)
    acc_scr[...] += _dot(h, wo_ref[...])

    @pl.when(c == pl.num_programs(1) - 1)
    def _():
        o_ref[...] = x_ref[...] + acc_scr[...]


def _ffn_chunk(f):
    for fc in (1408, 896, 512, 256, 128):
        if f % fc == 0:
            return fc
    return f


def _ffn(x, norm_w, w_in, w_out):
    N, D = x.shape
    F = w_out.shape[0]
    tm = min(N, 512)
    fc = _ffn_chunk(F)
    nfc = F // fc
    return pl.pallas_call(
        _ffn_kernel,
        grid=(N // tm, nfc),
        in_specs=[pl.BlockSpec((tm, D), lambda i, c: (i, 0)),
                  pl.BlockSpec((1, D), lambda i, c: (0, 0)),
                  pl.BlockSpec((D, fc), lambda i, c: (0, c)),
                  pl.BlockSpec((D, fc), lambda i, c: (0, nfc + c)),
                  pl.BlockSpec((fc, D), lambda i, c: (c, 0))],
        out_specs=pl.BlockSpec((tm, D), lambda i, c: (i, 0)),
        out_shape=jax.ShapeDtypeStruct((N, D), F32),
        scratch_shapes=[pltpu.VMEM((tm, D), BF16), pltpu.VMEM((tm, D), F32)],
        compiler_params=_cparams(("parallel", "arbitrary")),
        name="ffn",
    )(x, norm_w.reshape(1, D).astype(F32), w_in.astype(BF16), w_in.astype(BF16), w_out.astype(BF16))


def _kvq_kernel(x_ref, kvn_ref, qn_ref, wkv_ref, wq_ref, kg_ref, qg_ref, e_ref, et_ref,
                k_o, v_o, kb_o, vb_o, qb_o):
    e, et = e_ref[...], et_ref[...]
    x = x_ref[...]
    D = x.shape[1]
    xhat = x * lax.rsqrt(jnp.mean(x * x, axis=-1, keepdims=True) + RMS_EPS)
    kv = _dot((xhat * kvn_ref[...]).astype(BF16), wkv_ref[...])
    kraw = kv[:, :D]
    v = kv[:, D:]
    k = kraw * lax.rsqrt(_head_sum(kraw * kraw, e, et) * (1.0 / HEAD_DIM) + RMS_EPS) * kg_ref[...]
    q = _dot((xhat * qn_ref[...]).astype(BF16), wq_ref[...])
    q = q * lax.rsqrt(_head_sum(q * q, e, et) * (1.0 / HEAD_DIM) + RMS_EPS) * qg_ref[...]
    k_o[...] = k
    v_o[...] = v
    kb_o[...] = k.astype(BF16)
    vb_o[...] = v.astype(BF16)
    qb_o[...] = (q * (LOG2E / math.sqrt(HEAD_DIM))).astype(BF16)


def _kvq(x, p):
    N, D = x.shape
    H = D // HEAD_DIM
    tm = min(N, 512)
    e, et = _head_onehot(D)
    row = lambda a: a.reshape(1, D).astype(F32)
    consts = [row(p["kv_norm"]), row(p["norm_mix1"]), p["kv_w"].astype(BF16), p["sb_w_q"].astype(BF16),
              row(jnp.tile(p["kv_k_norm"], H)), row(jnp.tile(p["sb_q_norm"], H)), e, et]
    tile = pl.BlockSpec((tm, D), lambda i: (i, 0))
    f = jax.ShapeDtypeStruct((N, D), F32)
    h = jax.ShapeDtypeStruct((N, D), BF16)
    return pl.pallas_call(
        _kvq_kernel,
        grid=(N // tm,),
        in_specs=[tile] + [pl.BlockSpec(c.shape, lambda i: (0, 0)) for c in consts],
        out_specs=[tile] * 5,
        out_shape=[f, f, h, h, h],
        compiler_params=_cparams(("parallel",)),
        name="kvq_proj",
    )(x, *consts)


def _sb_prompt_kernel(bias_ref, q_ref, k_ref, v_ref, o_ref, acc_scr, car_scr, *, tq):
    j = pl.program_id(1)
    i = pl.program_id(2)
    q = q_ref[0]
    lane = lax.broadcasted_iota(jnp.int32, q.shape, 1)
    qh = [jnp.where(lane < HEAD_DIM, q, jnp.zeros_like(q)), jnp.where(lane >= HEAD_DIM, q, jnp.zeros_like(q))]
    ri = lax.broadcasted_iota(jnp.int32, (tq, tq), 0)
    ci = lax.broadcasted_iota(jnp.int32, (tq, tq), 1)
    later = (ri > ci).astype(BF16)
    causal = ci < ri
    rep = tq // LANES

    def kv(kb):
        start = pl.multiple_of(kb * tq, tq)
        return k_ref[0, pl.ds(start, tq), :], v_ref[0, pl.ds(start, tq), :]

    def logits(h, kblk):
        return _dot_nt(qh[h], kblk) + bias_ref[2 * j + h]

    def sums(z, masked):
        sp = _softplus2(z)
        if masked:
            sp = jnp.where(causal, sp, 0.0)
        tail = _dot(sp.astype(BF16), later)
        total = jnp.broadcast_to(tail[:, 0:1] + sp[:, 0:1], (tq, LANES))
        return z - sp - tail, total

    def scores(h, kblk, masked):
        return sums(logits(h, kblk), masked)

    def weights(part, carry, vblk, masked):
        pa = jnp.exp2(part if carry is None else part - jnp.tile(carry, (1, rep)))
        if masked:
            pa = jnp.where(causal, pa, 0.0)
        return _dot(pa.astype(BF16), vblk)

    kd, vd = kv(i)
    for h in range(2):
        part, total = scores(h, kd, True)
        acc_scr[h] = weights(part, None, vd, True)
        car_scr[h] = total

    def walk(first, n):
        blocks = [kv(first - t) for t in range(n)]
        zs = [[logits(h, kb) for kb, _ in blocks] for h in range(2)]
        ps = [[sums(z, False) for z in zs[h]] for h in range(2)]
        for h in range(2):
            carry = car_scr[h]
            out = None
            for (part, total), (_, vb) in zip(ps[h], blocks):
                w = weights(part, carry, vb, False)
                out = w if out is None else out + w
                carry = carry + total
            acc_scr[h] += out
            car_scr[h] = carry

    def group(s, _):
        walk(i - 1 - WALK * s, WALK)
        return 0

    lax.fori_loop(0, i // WALK, group, 0)
    rem = i % WALK
    step = WALK // 2
    while step >= 1:
        @pl.when((rem & step) != 0)
        def _(step=step):
            walk((rem & (2 * step - 1)) - 1, step)
        step //= 2

    o_ref[0] = jnp.where(lane < HEAD_DIM, acc_scr[0], acc_scr[1]).astype(o_ref.dtype)


def _sb_prompt(q, k, v, bias):
    B, T, D = q.shape
    NP = D // LANES
    tq = min(T, 256)
    grid_spec = pltpu.PrefetchScalarGridSpec(
        num_scalar_prefetch=1,
        grid=(B, NP, T // tq),
        in_specs=[pl.BlockSpec((1, tq, LANES), lambda b, j, i, bias: (b, i, j)),
                  pl.BlockSpec((1, T, LANES), lambda b, j, i, bias: (b, 0, j)),
                  pl.BlockSpec((1, T, LANES), lambda b, j, i, bias: (b, 0, j))],
        out_specs=pl.BlockSpec((1, tq, LANES), lambda b, j, i, bias: (b, i, j)),
        scratch_shapes=[pltpu.VMEM((2, tq, LANES), F32), pltpu.VMEM((2, tq, LANES), F32)],
    )
    return pl.pallas_call(
        functools.partial(_sb_prompt_kernel, tq=tq),
        grid_spec=grid_spec,
        out_shape=jax.ShapeDtypeStruct((B, T, D), BF16),
        compiler_params=_cparams(("parallel", "parallel", "arbitrary")),
        name="sb_attn_prompt",
    )(bias.astype(F32) * LOG2E, q, k, v)


def _sb_paged_kernel(pt_ref, q_ref, kn_ref, vn_ref, bias_ref, later_ref, *rest, tnew, page, heads, pps):
    k_refs, v_refs = rest[0:2 * pps:2], rest[1:2 * pps:2]
    o_ref, acc_scr, car_scr = rest[2 * pps:]
    s = pl.program_id(1)

    def segment(ks, vs, mask):
        nk = len(ks) * page
        q = q_ref[0]
        slab = 2 * tnew
        tiles = lambda refs, h: jnp.concatenate([r[0, h] for r in refs], axis=1).astype(BF16)
        zs = []
        for h in range(heads):
            qq = q[(h // 2) * slab:(h // 2 + 1) * slab, :]
            zz = _dot(qq, tiles(ks, h))
            zs.append(zz[(h % 2) * tnew:(h % 2 + 1) * tnew, :])
        z = jnp.concatenate(zs, axis=0) + bias_ref[:, :nk]
        sp = _softplus2(z)
        if mask is not None:
            sp = jnp.where(mask, sp, 0.0)
        tail = _dot(sp.astype(BF16), later_ref[:nk, :nk])
        carry = car_scr[...]
        pa = jnp.exp2(z - sp - tail - jnp.tile(carry, (1, nk // LANES)))
        if mask is not None:
            pa = jnp.where(mask, pa, 0.0)
        pa = pa.astype(BF16)
        for h in range(heads):
            res = _dot_nt(pa[(h // 2) * slab:(h // 2 + 1) * slab, :], tiles(vs, h))
            acc_scr[h * tnew:(h + 1) * tnew, :] += res[(h % 2) * tnew:(h % 2 + 1) * tnew, :]
        car_scr[...] = carry + jnp.broadcast_to(tail[:, 0:1] + sp[:, 0:1], carry.shape)

    @pl.when(s == 0)
    def _():
        acc_scr[...] = jnp.zeros_like(acc_scr)
        car_scr[...] = jnp.zeros_like(car_scr)
        key = lax.broadcasted_iota(jnp.int32, (LANES, page), 1)
        qi = lax.broadcasted_iota(jnp.int32, (LANES, page), 0) % tnew
        segment([kn_ref], [vn_ref], key < qi)

    @pl.when(s > 0)
    def _():
        segment(k_refs, v_refs, None)

    @pl.when(s == pl.num_programs(1) - 1)
    def _():
        o_ref[0] = acc_scr[...]


def _sb_paged(q, k_new, v_new, bias, cache_k, cache_v, page_table):
    B, tnew, D = q.shape
    H = D // HEAD_DIM
    n_pool, page = cache_k.shape[0], cache_k.shape[1]
    npg = page_table.shape[1]
    assert H * tnew == LANES and page == LANES and tnew % SUBLANES == 0
    ck = cache_k.transpose(0, 2, 3, 1)
    cv = cache_v.transpose(0, 2, 3, 1)
    new = lambda z: jnp.pad(z.reshape(B, tnew, H, HEAD_DIM).transpose(0, 2, 3, 1),
                            ((0, 0), (0, 0), (0, 0), (0, page - tnew)))
    qrows = q.reshape(B, tnew, H, HEAD_DIM).transpose(0, 2, 1, 3).reshape(B, LANES, HEAD_DIM)
    pps = next(n for n in (4, 2, 1) if npg % n == 0)
    nk = pps * page
    bias_rows = jnp.broadcast_to(jnp.repeat(bias.astype(F32) * LOG2E, tnew)[:, None], (LANES, nk))
    later = (jnp.arange(nk)[:, None] > jnp.arange(nk)[None, :]).astype(BF16)

    def page_map(slot):
        return lambda b, s, pt: (pt[b * npg + npg - jnp.maximum(s, 1) * pps + slot], 0, 0, 0)

    seq3 = lambda b, s, pt: (b, 0, 0)
    seq4 = lambda b, s, pt: (b, 0, 0, 0)
    const = lambda b, s, pt: (0, 0)
    pg = (1, H, HEAD_DIM, page)
    page_specs, page_args = [], []
    for slot in range(pps):
        page_specs += [pl.BlockSpec(pg, page_map(slot)), pl.BlockSpec(pg, page_map(slot))]
        page_args += [ck, cv]
    grid_spec = pltpu.PrefetchScalarGridSpec(
        num_scalar_prefetch=1,
        grid=(B, npg // pps + 1),
        in_specs=[pl.BlockSpec((1, LANES, HEAD_DIM), seq3),
                  pl.BlockSpec(pg, seq4), pl.BlockSpec(pg, seq4),
                  pl.BlockSpec((LANES, nk), const), pl.BlockSpec((nk, nk), const)] + page_specs,
        out_specs=pl.BlockSpec((1, LANES, HEAD_DIM), seq3),
        scratch_shapes=[pltpu.VMEM((LANES, HEAD_DIM), F32), pltpu.VMEM((LANES, LANES), F32)],
    )
    o = pl.pallas_call(
        functools.partial(_sb_paged_kernel, tnew=tnew, page=page, heads=H, pps=pps),
        grid_spec=grid_spec,
        out_shape=jax.ShapeDtypeStruct((B, LANES, HEAD_DIM), F32),
        compiler_params=_cparams(("parallel", "arbitrary")),
        name="sb_attn_paged",
    )(page_table.reshape(-1).astype(jnp.int32), qrows, new(k_new), new(v_new), bias_rows, later, *page_args)
    return o.reshape(B, H, tnew, HEAD_DIM).transpose(0, 2, 1, 3).reshape(B, tnew, D).astype(BF16)


def _attn_out_router_kernel(o_ref, x_ref, wo_ref, nw_ref, rt_ref, x_o, xn_o, route_o):
    x = x_ref[...] + _dot(o_ref[...], wo_ref[...])
    xn = _rms(x, nw_ref[...])
    x_o[...] = x
    xn_o[...] = xn
    n_exp = rt_ref.shape[1]
    logits = _mm(xn, rt_ref[...])
    lane = lax.broadcasted_iota(jnp.int32, logits.shape, 1).astype(F32)
    v1 = jnp.max(logits, axis=-1, keepdims=True)
    i1 = jnp.min(jnp.where(logits == v1, lane, float(n_exp)), axis=-1, keepdims=True)
    rest = jnp.where(lane == i1, -jnp.inf, logits)
    v2 = jnp.max(rest, axis=-1, keepdims=True)
    i2 = jnp.min(jnp.where(rest == v2, lane, float(n_exp)), axis=-1, keepdims=True)
    e2 = jnp.exp(v2 - v1)
    g1 = 1.0 / (1.0 + e2)
    g2 = e2 * g1
    col = lax.broadcasted_iota(jnp.int32, (x.shape[0], ROUTE_W), 1)
    route_o[...] = (jnp.where(col == 0, i1, 0.0) + jnp.where(col == 1, i2, 0.0)
                    + jnp.where(col == 2, g1, 0.0) + jnp.where(col == 3, g2, 0.0))


def _attn_out_router(o, x, p):
    N, D = x.shape
    tm = min(N, 512)
    E = p["moe_router"].shape[1]
    router = p["moe_router"].astype(F32)
    tile = pl.BlockSpec((tm, D), lambda i: (i, 0))
    return pl.pallas_call(
        _attn_out_router_kernel,
        grid=(N // tm,),
        in_specs=[tile, tile, pl.BlockSpec((D, D), lambda i: (0, 0)), pl.BlockSpec((1, D), lambda i: (0, 0)),
                  pl.BlockSpec((D, E), lambda i: (0, 0))],
        out_specs=[tile, tile, pl.BlockSpec((tm, ROUTE_W), lambda i: (i, 0))],
        out_shape=[jax.ShapeDtypeStruct((N, D), F32), jax.ShapeDtypeStruct((N, D), F32),
                   jax.ShapeDtypeStruct((N, ROUTE_W), F32)],
        compiler_params=_cparams(("parallel",)),
        name="attn_out_router",
    )(o, x, p["sb_w_o"].astype(BF16), p["norm_ffn1"].reshape(1, D).astype(F32), router)


def _moe_plan(route, n_exp, tm, n_rows):
    n = route.shape[0]
    eid = jnp.concatenate([route[:, 0], route[:, 1]]).astype(jnp.int32)
    onehot = (eid[:, None] == jnp.arange(n_exp, dtype=jnp.int32)[None, :]).astype(jnp.int32)
    cum = jnp.cumsum(onehot, axis=0)
    counts = cum[-1]
    rank = jnp.sum(onehot * (cum - 1), axis=1)
    padded = (counts + tm - 1) // tm * tm
    ends = jnp.cumsum(padded)
    dest = (ends - padded)[eid] + rank
    src = jnp.zeros((n_rows,), jnp.int32).at[dest].set(jnp.tile(jnp.arange(n, dtype=jnp.int32), TOP_K))
    tile_start = jnp.arange(n_rows // tm, dtype=jnp.int32) * tm
    tile_exp = jnp.minimum(jnp.searchsorted(ends, tile_start, side="right"), n_exp - 1).astype(jnp.int32)
    active = (tile_start < ends[-1]).astype(jnp.int32)
    return src, dest, tile_exp, active


def _row_copy(src_hbm, row, dst, slot, sem):
    return pltpu.make_async_copy(src_hbm.at[pl.ds(row, 1), :], dst.at[pl.ds(slot, 1), :], sem)


def _moe_gather_kernel(src_ref, x_hbm, o_ref, sem):
    rows = o_ref.shape[0]

    def issue(r, carry):
        _row_copy(x_hbm, src_ref[r // LANES, r % LANES], o_ref, r, sem).start()
        return carry

    lax.fori_loop(0, rows, issue, 0, unroll=8)
    pltpu.make_async_copy(x_hbm.at[pl.ds(0, rows), :], o_ref, sem).wait()


def _moe_gather(xn, src, gt):
    n_rows = src.shape[0]
    D = xn.shape[1]
    return pl.pallas_call(
        _moe_gather_kernel,
        grid=(n_rows // gt,),
        in_specs=[pl.BlockSpec((gt // LANES, LANES), lambda i: (i, 0), memory_space=pltpu.SMEM),
                  pl.BlockSpec(memory_space=pl.ANY)],
        out_specs=pl.BlockSpec((gt, D), lambda i: (i, 0)),
        out_shape=jax.ShapeDtypeStruct((n_rows, D), xn.dtype),
        scratch_shapes=[pltpu.SemaphoreType.DMA(())],
        compiler_params=_cparams(("arbitrary",)),
        name="moe_gather",
    )(src.reshape(n_rows // LANES, LANES), xn)


def _moe_experts_kernel(te_ref, act_ref, xs_ref, wg_ref, wu_ref, wo_ref, o_ref, acc_scr):
    i = pl.program_id(0)
    c = pl.program_id(1)

    @pl.when(c == 0)
    def _():
        acc_scr[...] = jnp.zeros_like(acc_scr)

    @pl.when(act_ref[i] > 0)
    def _():
        xs = xs_ref[...].astype(BF16)
        gate = _dot(xs, wg_ref[0])
        up = _dot(xs, wu_ref[0])
        h = (gate * _sigmoid(gate) * up).astype(BF16)
        acc_scr[...] += _dot(h, wo_ref[0])

    @pl.when(c == pl.num_programs(1) - 1)
    def _():
        o_ref[...] = acc_scr[...]


def _moe_experts(xs, tile_exp, active, w_in, w_out, tm):
    n_rows, D = xs.shape
    F = w_out.shape[1]
    fc = _ffn_chunk(F)
    nfc = F // fc
    wi = w_in.astype(BF16)
    grid_spec = pltpu.PrefetchScalarGridSpec(
        num_scalar_prefetch=2,
        grid=(n_rows // tm, nfc),
        in_specs=[pl.BlockSpec((tm, D), lambda i, c, te, act: (i, 0)),
                  pl.BlockSpec((1, D, fc), lambda i, c, te, act: (te[i], 0, c * act[i])),
                  pl.BlockSpec((1, D, fc), lambda i, c, te, act: (te[i], 0, nfc + c * act[i])),
                  pl.BlockSpec((1, fc, D), lambda i, c, te, act: (te[i], c * act[i], 0))],
        out_specs=pl.BlockSpec((tm, D), lambda i, c, te, act: (i, 0)),
        scratch_shapes=[pltpu.VMEM((tm, D), F32)],
    )
    return pl.pallas_call(
        _moe_experts_kernel,
        grid_spec=grid_spec,
        out_shape=jax.ShapeDtypeStruct((n_rows, D), F32),
        compiler_params=_cparams(("parallel", "arbitrary")),
        name="moe_experts",
    )(tile_exp, active, xs, wi, wi, w_out.astype(BF16))


def _moe_combine_kernel(pos_ref, x_ref, route_ref, ys_hbm, o_ref, buf, sem):
    tm = x_ref.shape[0]

    def issue(t, carry):
        for k in range(TOP_K):
            _row_copy(ys_hbm, pos_ref[k, t // LANES, t % LANES], buf.at[k], t, sem).start()
        return carry

    lax.fori_loop(0, tm, issue, 0, unroll=4)
    for k in range(TOP_K):
        pltpu.make_async_copy(ys_hbm.at[pl.ds(0, tm), :], buf.at[k], sem).wait()
    route = route_ref[...]
    o_ref[...] = x_ref[...] + route[:, 2:3] * buf[0] + route[:, 3:4] * buf[1]


def _moe_combine(x, route, ys, dest, ct):
    N, D = x.shape
    pos = dest.reshape(TOP_K, N // LANES, LANES)
    return pl.pallas_call(
        _moe_combine_kernel,
        grid=(N // ct,),
        in_specs=[pl.BlockSpec((TOP_K, ct // LANES, LANES), lambda i: (0, i, 0), memory_space=pltpu.SMEM),
                  pl.BlockSpec((ct, D), lambda i: (i, 0)),
                  pl.BlockSpec((ct, ROUTE_W), lambda i: (i, 0)),
                  pl.BlockSpec(memory_space=pl.ANY)],
        out_specs=pl.BlockSpec((ct, D), lambda i: (i, 0)),
        out_shape=jax.ShapeDtypeStruct((N, D), F32),
        scratch_shapes=[pltpu.VMEM((TOP_K, ct, D), F32), pltpu.SemaphoreType.DMA(())],
        compiler_params=_cparams(("arbitrary",)),
        name="moe_combine",
    )(pos, x, route, ys)


def _moe(x, xn, route, w_in, w_out):
    N, D = x.shape
    E = w_out.shape[0]
    tm = min(N, 512)
    gt = SUBLANES * LANES
    assert gt % tm == 0
    n_rows = -(-(TOP_K * N + E * tm) // gt) * gt
    src, dest, tile_exp, active = _moe_plan(route, E, tm, n_rows)
    xs = _moe_gather(xn, src, gt)
    ys = _moe_experts(xs, tile_exp, active, w_in, w_out, tm)
    return _moe_combine(x, route, ys, dest, min(N, gt))


def _decoder_group(x, shift0, wkv0, past, p):
    B, T, D = x.shape
    N = B * T
    r, lw, k, v, a, b, g, bonus, last = _rwkv_pre(x, shift0, p)
    y, s_fin = _rwkv_scan(r, lw, k, v, a, b, wkv0)
    x1 = _rwkv_post(y.reshape(N, D), bonus.reshape(N, D), g.reshape(N, D), x.reshape(N, D), p)
    x2 = _ffn(x1, p["norm_ffn0"], p["ffn_w_in"], p["ffn_w_out"])
    k_sh, v_sh, kb, vb, qb = _kvq(x2, p)
    sh3 = lambda z: z.reshape(B, T, D)
    if past is None:
        o = _sb_prompt(sh3(qb), sh3(kb), sh3(vb), p["sb_bias"])
    else:
        o = _sb_paged(sh3(qb), sh3(k_sh), sh3(v_sh), p["sb_bias"], *past)
    x3, xn3, route = _attn_out_router(o.reshape(N, D), x2, p)
    x4 = _moe(x3, xn3, route, p["moe_w_in"], p["moe_w_out"])
    H = D // HEAD_DIM
    return (x4.reshape(B, T, D), last.reshape(1, B, D), s_fin[None], k_sh.reshape(B, T, H, HEAD_DIM),
            v_sh.reshape(B, T, H, HEAD_DIM))


def kernel(x_prompt, x_sample, cache_k, cache_v, state_wkv, state_shift, page_table, norm_mix, norm_ffn, rw_mix, rw_w_r, rw_w_k, rw_w_v, rw_w_o, rw_w0, rw_w1, rw_w2, rw_a0, rw_a1, rw_a2, rw_g1, rw_g2, rw_k_k, rw_k_a, rw_r_k, rw_ln_w, rw_ln_b, kv_norm, kv_w, kv_k_norm, sb_w_q, sb_q_norm, sb_bias, sb_w_o, ffn_w_in, ffn_w_out, moe_router, moe_w_in, moe_w_out):
    assert norm_mix.shape[0] == 2 and state_wkv.shape[0] == 1, "one RWKV layer followed by one attention layer"
    p = dict(norm_mix0=norm_mix[0], norm_mix1=norm_mix[1], norm_ffn0=norm_ffn[0], norm_ffn1=norm_ffn[1],
             rw_mix=rw_mix[0], rw_w_r=rw_w_r[0], rw_w_k=rw_w_k[0], rw_w_v=rw_w_v[0], rw_w_o=rw_w_o[0],
             rw_w0=rw_w0[0], rw_w1=rw_w1[0], rw_w2=rw_w2[0], rw_a0=rw_a0[0], rw_a1=rw_a1[0], rw_a2=rw_a2[0],
             rw_g1=rw_g1[0], rw_g2=rw_g2[0], rw_k_k=rw_k_k[0], rw_k_a=rw_k_a[0], rw_r_k=rw_r_k[0].reshape(-1),
             rw_ln_w=rw_ln_w[0], rw_ln_b=rw_ln_b[0], kv_norm=kv_norm, kv_w=kv_w, kv_k_norm=kv_k_norm,
             sb_w_q=sb_w_q[0], sb_q_norm=sb_q_norm[0], sb_bias=sb_bias[0], sb_w_o=sb_w_o[0],
             ffn_w_in=ffn_w_in[0], ffn_w_out=ffn_w_out[0], moe_router=moe_router[0], moe_w_in=moe_w_in[0],
             moe_w_out=moe_w_out[0])
    bp, _, d = x_prompt.shape
    h = d // HEAD_DIM
    y_p, shift_p, wkv_p, k_p, v_p = _decoder_group(
        x_prompt, jnp.zeros((bp, d), x_prompt.dtype), jnp.zeros((bp, h, HEAD_DIM, HEAD_DIM), x_prompt.dtype), None, p)
    y_s, shift_s, wkv_s, k_s, v_s = _decoder_group(
        x_sample, state_shift[0], state_wkv[0], (cache_k, cache_v, page_table), p)
    return (y_p, y_s, wkv_p, shift_p, k_p, v_p, wkv_s, shift_s, k_s, v_s)
```

```python
import functools
import math

import jax
import jax.numpy as jnp
from jax import lax
from jax.experimental import pallas as pl
from jax.experimental.pallas import tpu as pltpu

F32 = jnp.float32
BF16 = jnp.bfloat16

HEAD_DIM = 64
LANES = 128
SUBLANES = 8
RMS_EPS = 1e-6
RW_GN_EPS = 64e-5
TOP_K = 2
ROUTE_W = 8
LOG2E = math.log2(math.e)
WALK = 4
SCAN_CHUNK = 64
VMEM_LIMIT = 52 * 1024 * 1024


def _cparams(sem):
    return pltpu.CompilerParams(dimension_semantics=sem, vmem_limit_bytes=VMEM_LIMIT)


def _dot(a, b):
    return jnp.dot(a, b, preferred_element_type=F32)


def _dot_nt(a, b):
    return lax.dot_general(a, b, (((1,), (1,)), ((), ())), preferred_element_type=F32)


def _dot_tn(a, b):
    return lax.dot_general(a, b, (((0,), (0,)), ((), ())), preferred_element_type=F32)


def _split(x, n):
    parts = []
    rem = x
    for i in range(n):
        p = rem.astype(BF16)
        parts.append(p)
        if i + 1 < n:
            rem = rem - p.astype(F32)
    return parts


def _mm(a, b, kind="nn", na=3, nb=3):
    f = {"nn": _dot, "nt": _dot_nt, "tn": _dot_tn}[kind]
    ap = _split(a, na) if a.dtype != BF16 else [a]
    bp = _split(b, nb) if b.dtype != BF16 else [b]
    keep = max(len(ap), len(bp))
    out = None
    for i, x in enumerate(ap):
        for j, y in enumerate(bp):
            if i + j < keep:
                t = f(x, y)
                out = t if out is None else out + t
    return out


def _rms(x, gain):
    return x * lax.rsqrt(jnp.mean(x * x, axis=-1, keepdims=True) + RMS_EPS) * gain


def _softplus(z):
    return jnp.maximum(z, 0.0) + jnp.log(1.0 + jnp.exp(-jnp.abs(z)))


def _softplus2(z):
    neg_abs = lax.bitcast_convert_type(lax.bitcast_convert_type(z, jnp.uint32) | jnp.uint32(0x80000000), F32)
    return jnp.maximum(z, 0.0) + jnp.log(1.0 + jnp.exp2(neg_abs)) * LOG2E


def _sigmoid(z):
    return 1.0 / (1.0 + jnp.exp(-z))


def _head_sum(x, e, et):
    s = _mm(x, e, na=2)
    return _mm(s, et, na=2)


def _head_onehot(d):
    h = d // HEAD_DIM
    e = (jnp.arange(d)[:, None] // HEAD_DIM == jnp.arange(LANES)[None, :]).astype(BF16)
    del h
    return e, e.T


def _rwkv_pre_kernel(x_ref, xp_ref, sh_ref, nm_ref, mix_ref, wr_ref, wk_ref, wv_ref, w0_ref, w1_ref,
                     w2_ref, a0_ref, a1_ref, a2_ref, g1_ref, g2_ref, kk_ref, ka_ref, rk_ref, e_ref,
                     et_ref, r_o, lw_o, k_o, v_o, a_o, b_o, g_o, bonus_o, last_o):
    t = pl.program_id(1)
    gain = nm_ref[...]
    x = x_ref[0]
    tt = x.shape[0]
    xn = _rms(x, gain)
    prev_tile_last = _rms(xp_ref[0][SUBLANES - 1:SUBLANES, :], gain)
    first = jnp.where(t == 0, sh_ref[0], prev_tile_last)
    row = lax.broadcasted_iota(jnp.int32, xn.shape, 0)
    prev = jnp.where(row == 0, first, pltpu.roll(xn, 1, 0))
    dx = prev - xn

    def mixed(i):
        return (xn + dx * mix_ref[i:i + 1, :]).astype(BF16)

    r = _dot(mixed(0), wr_ref[...])
    lw_raw = w0_ref[...] + _dot(jnp.tanh(_dot(mixed(1), w1_ref[...])).astype(BF16), w2_ref[...])
    k = _dot(mixed(2), wk_ref[...])
    v = _dot(mixed(3), wv_ref[...])
    asig = _sigmoid(a0_ref[...] + _dot(_dot(mixed(4), a1_ref[...]).astype(BF16), a2_ref[...]))
    g = _dot(_sigmoid(_dot(mixed(5), g1_ref[...])).astype(BF16), g2_ref[...])

    log_w = -_softplus(-lw_raw) - 0.5
    lw_o[0] = -jnp.exp(log_w)
    e, et = e_ref[...], et_ref[...]
    kkr = k * kk_ref[...]
    nrm = jnp.maximum(jnp.sqrt(_head_sum(kkr * kkr, e, et)), 1e-12)
    kk = kkr * (1.0 / nrm)
    kmod = k * (1.0 + (asig - 1.0) * ka_ref[...])
    r_o[0] = r
    k_o[0] = kmod
    v_o[0] = v
    a_o[0] = -kk
    b_o[0] = kk * asig
    g_o[0] = g
    bonus_o[0] = _head_sum(r * kmod * rk_ref[...], e, et) * v
    last_o[0] = xn[tt - 1:tt, :]


def _rwkv_pre(x, shift0, p):
    B, T, D = x.shape
    tt = min(T, 256)
    nt = T // tt
    e, et = _head_onehot(D)
    row = lambda a: a.reshape(1, -1).astype(F32)
    bf = lambda a: a.astype(BF16)
    consts = [row(p["norm_mix0"]), p["rw_mix"].astype(F32), bf(p["rw_w_r"]), bf(p["rw_w_k"]), bf(p["rw_w_v"]),
              row(p["rw_w0"]), bf(p["rw_w1"]), bf(p["rw_w2"]), row(p["rw_a0"]), bf(p["rw_a1"]), bf(p["rw_a2"]),
              bf(p["rw_g1"]), bf(p["rw_g2"]), row(p["rw_k_k"]), row(p["rw_k_a"]), row(p["rw_r_k"]), e, et]
    const_specs = [pl.BlockSpec(c.shape, lambda b, t, n=c.ndim: (0,) * n) for c in consts]
    tile = pl.BlockSpec((1, tt, D), lambda b, t: (b, t, 0))
    prev8 = pl.BlockSpec((1, SUBLANES, D), lambda b, t: (b, jnp.maximum(t * (tt // SUBLANES) - 1, 0), 0))
    one = pl.BlockSpec((1, 1, D), lambda b, t: (b, 0, 0))
    big = jax.ShapeDtypeStruct((B, T, D), F32)
    outs = pl.pallas_call(
        _rwkv_pre_kernel,
        grid=(B, nt),
        in_specs=[tile, prev8, one] + const_specs,
        out_specs=[tile] * 8 + [one],
        out_shape=[big] * 8 + [jax.ShapeDtypeStruct((B, 1, D), F32)],
        compiler_params=_cparams(("parallel", "arbitrary")),
        name="rwkv_pre",
    )(x, x, shift0.reshape(B, 1, D), *consts)
    return outs


def _scan_kernel(r_ref, lw_ref, k_ref, v_ref, a_ref, b_ref, s0_ref, y_ref, sT_ref, s_scr, *, nchunk, npair):
    C = SCAN_CHUNK
    c = pl.program_id(2)

    @pl.when(c == 0)
    def _():
        s_scr[...] = s0_ref[0]

    row2 = lax.broadcasted_iota(jnp.int32, (2 * C, LANES), 0)
    lane2 = lax.broadcasted_iota(jnp.int32, (2 * C, LANES), 1)
    own = (lane2 // HEAD_DIM) == (row2 // C)
    ri = lax.broadcasted_iota(jnp.int32, (2 * C, 2 * C), 0)
    ci = lax.broadcasted_iota(jnp.int32, (2 * C, 2 * C), 1)
    strict = ci < ri
    incl = ci <= ri
    eye = (ci == ri).astype(F32)
    tri = (lax.broadcasted_iota(jnp.int32, (C, C), 1) <= lax.broadcasted_iota(jnp.int32, (C, C), 0)).astype(BF16)

    def stack(x):
        return jnp.where(own, jnp.concatenate([x, x], axis=0), 0.0).astype(BF16)

    def mm(x, y, kind="nn"):
        return _mm(x.astype(BF16), y.astype(BF16), kind)

    chains = [(p, g) for p in range(npair) for g in range(nchunk)]
    w = {}
    for ch in chains:
        p, g = ch
        sl, ln = pl.ds(g * C, C), pl.ds(p * LANES, LANES)
        r, lw, k, v, a, b = (ref[0, sl, ln] for ref in (r_ref, lw_ref, k_ref, v_ref, a_ref, b_ref))
        gc = _mm(tri, lw, nb=2)
        gt = gc[C - 1:C, :]
        eneg = jnp.exp(-gc)
        efar = jnp.exp(gt - gc)
        w[ch] = dict(rt=stack(r * jnp.exp(gc)), at=stack(a * jnp.exp(gc - lw)), bt=stack(b * eneg),
                     kt=stack(k * eneg), bw=stack(b * efar), kw=stack(k * efar), vs=stack(v), dec=jnp.exp(gt))
    for ch in chains:
        c_ = w[ch]
        G = mm(jnp.concatenate([c_["at"], c_["rt"]], axis=0), jnp.concatenate([c_["bt"], c_["kt"]], axis=0), "nt")
        c_["L"] = jnp.where(strict, G[:2 * C, :2 * C], 0.0)
        c_["Lak"] = jnp.where(strict, G[:2 * C, 2 * C:], 0.0)
        c_["RB"] = jnp.where(incl, G[2 * C:, :2 * C], 0.0).astype(BF16)
        c_["RK"] = jnp.where(incl, G[2 * C:, 2 * C:], 0.0)
    for ch in chains:
        c_ = w[ch]
        c_["Tinv"] = eye + c_["L"]
        c_["A"] = mm(c_["L"], c_["L"])
        c_["LV"] = mm(c_["Lak"], c_["vs"]).astype(BF16)
        c_["Y0"] = mm(c_["RK"], c_["vs"])
        c_["Z0"] = mm(c_["vs"], c_["kw"], "tn")
    for _ in range(int(math.log2(C)) - 2):
        for ch in chains:
            c_ = w[ch]
            both = mm(c_["A"], jnp.concatenate([c_["A"], c_["Tinv"]], axis=1))
            c_["A"] = both[:, :2 * C]
            c_["Tinv"] = c_["Tinv"] + both[:, 2 * C:]
    for ch in chains:
        c_ = w[ch]
        c_["Tinv"] = c_["Tinv"] + mm(c_["A"], c_["Tinv"])
    for ch in chains:
        c_ = w[ch]
        PQ = mm(c_["Tinv"], jnp.concatenate([c_["at"], c_["LV"]], axis=1))
        c_["P"] = PQ[:, :LANES].astype(BF16)
        c_["QT"] = PQ[:, LANES:].T
    S = [s_scr[p] for p in range(npair)]
    for g in range(nchunk):
        for p in range(npair):
            c_ = w[(p, g)]
            c_["S"] = S[p].astype(BF16)
            UT = mm(c_["S"], c_["P"], "nt") + c_["QT"]
            c_["UT"] = UT
            S[p] = S[p] * c_["dec"] + mm(UT, c_["bw"]) + c_["Z0"]
    for p in range(npair):
        s_scr[p] = S[p]
    for ch in chains:
        p, g = ch
        c_ = w[ch]
        Y = mm(c_["rt"], c_["S"], "nt") + mm(c_["RB"], c_["UT"].T) + c_["Y0"]
        y_ref[0, pl.ds(g * C, C), pl.ds(p * LANES, LANES)] = Y[:C] + Y[C:]

    @pl.when(c == pl.num_programs(2) - 1)
    def _():
        sT_ref[0] = s_scr[...]


def _rwkv_scan(r, lw, k, v, a, b, s0):
    B, T, D = r.shape
    H = D // HEAD_DIM
    NP = D // LANES
    C = SCAN_CHUNK
    Tp = -(-T // C) * C
    if Tp != T:
        pad = lambda z: jnp.pad(z, ((0, 0), (0, Tp - T), (0, 0)))
        r, lw, k, v, a, b = (pad(z) for z in (r, lw, k, v, a, b))
    nchunk = next(n for n in (8, 4, 1) if Tp % (n * C) == 0)
    npair = next(n for n in (8, 4, 2, 1) if NP % n == 0 and n * nchunk <= 16)
    tb = nchunk * C
    s0p = s0.astype(F32).reshape(B, NP, 2, HEAD_DIM, HEAD_DIM)
    z = jnp.zeros_like(s0p[:, :, 0])
    s0bd = jnp.concatenate([jnp.concatenate([s0p[:, :, 0], z], axis=-1),
                            jnp.concatenate([z, s0p[:, :, 1]], axis=-1)], axis=-2)
    tile = pl.BlockSpec((1, tb, npair * LANES), lambda bb, j, c: (bb, c, j))
    st = pl.BlockSpec((1, npair, LANES, LANES), lambda bb, j, c: (bb, j, 0, 0))
    y, sT = pl.pallas_call(
        functools.partial(_scan_kernel, nchunk=nchunk, npair=npair),
        grid=(B, NP // npair, Tp // tb),
        in_specs=[tile] * 6 + [st],
        out_specs=[tile, st],
        out_shape=[jax.ShapeDtypeStruct((B, Tp, D), F32), jax.ShapeDtypeStruct((B, NP, LANES, LANES), F32)],
        scratch_shapes=[pltpu.VMEM((npair, LANES, LANES), F32)],
        compiler_params=_cparams(("parallel", "parallel", "arbitrary")),
        name="rwkv_scan",
    )(r, lw, k, v, a, b, s0bd)
    s_fin = jnp.stack([sT[:, :, :HEAD_DIM, :HEAD_DIM], sT[:, :, HEAD_DIM:, HEAD_DIM:]], axis=2)
    return y[:, :T], s_fin.reshape(B, H, HEAD_DIM, HEAD_DIM)


def _rwkv_post_kernel(y_ref, bonus_ref, g_ref, x_ref, lnw_ref, lnb_ref, wo_ref, e_ref, et_ref, o_ref):
    e, et = e_ref[...], et_ref[...]
    y = y_ref[...]
    mu = _head_sum(y, e, et) * (1.0 / HEAD_DIM)
    d = y - mu
    var = _head_sum(d * d, e, et) * (1.0 / HEAD_DIM)
    yn = d * lax.rsqrt(var + RW_GN_EPS) * lnw_ref[...] + lnb_ref[...]
    z = ((yn + bonus_ref[...]) * g_ref[...]).astype(BF16)
    o_ref[...] = x_ref[...] + _dot(z, wo_ref[...])


def _rwkv_post(y, bonus, g, x, p):
    N, D = x.shape
    tm = min(N, 512)
    e, et = _head_onehot(D)
    consts = [p["rw_ln_w"].reshape(1, D).astype(F32), p["rw_ln_b"].reshape(1, D).astype(F32),
              p["rw_w_o"].astype(BF16), e, et]
    tile = pl.BlockSpec((tm, D), lambda i: (i, 0))
    return pl.pallas_call(
        _rwkv_post_kernel,
        grid=(N // tm,),
        in_specs=[tile] * 4 + [pl.BlockSpec(c.shape, lambda i: (0, 0)) for c in consts],
        out_specs=tile,
        out_shape=jax.ShapeDtypeStruct((N, D), F32),
        compiler_params=_cparams(("parallel",)),
        name="rwkv_post",
    )(y, bonus, g, x, *consts)


def _ffn_kernel(x_ref, nw_ref, wg_ref, wu_ref, wo_ref, o_ref, xn_scr, acc_scr):
    c = pl.program_id(1)

    @pl.when(c == 0)
    def _():
        xn_scr[...] = _rms(x_ref[...], nw_ref[...]).astype(BF16)
        acc_scr[...] = jnp.zeros_like(acc_scr)

    xn = xn_scr[...]
    gate = _dot(xn, wg_ref[...])
    up = _dot(xn, wu_ref[...])
    h = (gate * _sigmoid(gate) * up).astype(BF16)
    acc_scr[...] += _dot(h, wo_ref[...])

    @pl.when(c == pl.num_programs(1) - 1)
    def _():
        o_ref[...] = x_ref[...] + acc_scr[...]


def _ffn_chunk(f):
    for fc in (1408, 896, 512, 256, 128):
        if f % fc == 0:
            return fc
    return f


def _ffn(x, norm_w, w_in, w_out):
    N, D = x.shape
    F = w_out.shape[0]
    tm = min(N, 512)
    fc = _ffn_chunk(F)
    nfc = F // fc
    return pl.pallas_call(
        _ffn_kernel,
        grid=(N // tm, nfc),
        in_specs=[pl.BlockSpec((tm, D), lambda i, c: (i, 0)),
                  pl.BlockSpec((1, D), lambda i, c: (0, 0)),
                  pl.BlockSpec((D, fc), lambda i, c: (0, c)),
                  pl.BlockSpec((D, fc), lambda i, c: (0, nfc + c)),
                  pl.BlockSpec((fc, D), lambda i, c: (c, 0))],
        out_specs=pl.BlockSpec((tm, D), lambda i, c: (i, 0)),
        out_shape=jax.ShapeDtypeStruct((N, D), F32),
        scratch_shapes=[pltpu.VMEM((tm, D), BF16), pltpu.VMEM((tm, D), F32)],
        compiler_params=_cparams(("parallel", "arbitrary")),
        name="ffn",
    )(x, norm_w.reshape(1, D).astype(F32), w_in.astype(BF16), w_in.astype(BF16), w_out.astype(BF16))


def _kvq_kernel(x_ref, kvn_ref, qn_ref, wkv_ref, wq_ref, kg_ref, qg_ref, e_ref, et_ref,
                k_o, v_o, kb_o, vb_o, qb_o):
    e, et = e_ref[...], et_ref[...]
    x = x_ref[...]
    D = x.shape[1]
    xhat = x * lax.rsqrt(jnp.mean(x * x, axis=-1, keepdims=True) + RMS_EPS)
    kv = _dot((xhat * kvn_ref[...]).astype(BF16), wkv_ref[...])
    kraw = kv[:, :D]
    v = kv[:, D:]
    k = kraw * lax.rsqrt(_head_sum(kraw * kraw, e, et) * (1.0 / HEAD_DIM) + RMS_EPS) * kg_ref[...]
    q = _dot((xhat * qn_ref[...]).astype(BF16), wq_ref[...])
    q = q * lax.rsqrt(_head_sum(q * q, e, et) * (1.0 / HEAD_DIM) + RMS_EPS) * qg_ref[...]
    k_o[...] = k
    v_o[...] = v
    kb_o[...] = k.astype(BF16)
    vb_o[...] = v.astype(BF16)
    qb_o[...] = (q * (LOG2E / math.sqrt(HEAD_DIM))).astype(BF16)


def _kvq(x, p):
    N, D = x.shape
    H = D // HEAD_DIM
    tm = min(N, 512)
    e, et = _head_onehot(D)
    row = lambda a: a.reshape(1, D).astype(F32)
    consts = [row(p["kv_norm"]), row(p["norm_mix1"]), p["kv_w"].astype(BF16), p["sb_w_q"].astype(BF16),
              row(jnp.tile(p["kv_k_norm"], H)), row(jnp.tile(p["sb_q_norm"], H)), e, et]
    tile = pl.BlockSpec((tm, D), lambda i: (i, 0))
    f = jax.ShapeDtypeStruct((N, D), F32)
    h = jax.ShapeDtypeStruct((N, D), BF16)
    return pl.pallas_call(
        _kvq_kernel,
        grid=(N // tm,),
        in_specs=[tile] + [pl.BlockSpec(c.shape, lambda i: (0, 0)) for c in consts],
        out_specs=[tile] * 5,
        out_shape=[f, f, h, h, h],
        compiler_params=_cparams(("parallel",)),
        name="kvq_proj",
    )(x, *consts)


def _sb_prompt_kernel(bias_ref, q_ref, k_ref, v_ref, o_ref, acc_scr, car_scr, *, tq):
    j = pl.program_id(1)
    i = pl.program_id(2)
    q = q_ref[0]
    lane = lax.broadcasted_iota(jnp.int32, q.shape, 1)
    qh = [jnp.where(lane < HEAD_DIM, q, jnp.zeros_like(q)), jnp.where(lane >= HEAD_DIM, q, jnp.zeros_like(q))]
    ri = lax.broadcasted_iota(jnp.int32, (tq, tq), 0)
    ci = lax.broadcasted_iota(jnp.int32, (tq, tq), 1)
    later = (ri > ci).astype(BF16)
    causal = ci < ri
    rep = tq // LANES

    def kv(kb):
        start = pl.multiple_of(kb * tq, tq)
        return k_ref[0, pl.ds(start, tq), :], v_ref[0, pl.ds(start, tq), :]

    def logits(h, kblk):
        return _dot_nt(qh[h], kblk) + bias_ref[2 * j + h]

    def sums(z, masked):
        sp = _softplus2(z)
        if masked:
            sp = jnp.where(causal, sp, 0.0)
        tail = _dot(sp.astype(BF16), later)
        total = jnp.broadcast_to(tail[:, 0:1] + sp[:, 0:1], (tq, LANES))
        return z - sp - tail, total

    def weights(part, carry, vblk, masked):
        pa = jnp.exp2(part - jnp.tile(carry, (1, rep)))
        if masked:
            pa = jnp.where(causal, pa, 0.0)
        return _dot(pa.astype(BF16), vblk)

    def walk(first, n, diagonal):
        blocks = [kv(first - t) for t in range(n)]
        masks = [diagonal and t == 0 for t in range(n)]
        zs = [[logits(h, kb) for kb, _ in blocks] for h in range(2)]
        ps = [[sums(z, m) for z, m in zip(zs[h], masks)] for h in range(2)]
        for h in range(2):
            carry = car_scr[h]
            out = None
            for (part, total), (_, vb), m in zip(ps[h], blocks, masks):
                w = weights(part, carry, vb, m)
                out = w if out is None else out + w
                carry = carry + total
            acc_scr[h] += out
            car_scr[h] = carry

    acc_scr[...] = jnp.zeros_like(acc_scr)
    car_scr[...] = jnp.zeros_like(car_scr)
    head = (i + 1) % WALK
    for n in range(1, WALK + 1):
        @pl.when(head == n % WALK)
        def _(n=n):
            walk(i, n, True)
    first_rest = i - jnp.where(head == 0, WALK, head)

    def group(s, _):
        walk(first_rest - WALK * s, WALK, False)
        return 0

    lax.fori_loop(0, (first_rest + 1) // WALK, group, 0)

    o_ref[0] = jnp.where(lane < HEAD_DIM, acc_scr[0], acc_scr[1]).astype(o_ref.dtype)


def _sb_prompt(q, k, v, bias):
    B, T, D = q.shape
    NP = D // LANES
    tq = min(T, 256)
    grid_spec = pltpu.PrefetchScalarGridSpec(
        num_scalar_prefetch=1,
        grid=(B, NP, T // tq),
        in_specs=[pl.BlockSpec((1, tq, LANES), lambda b, j, i, bias: (b, i, j)),
                  pl.BlockSpec((1, T, LANES), lambda b, j, i, bias: (b, 0, j)),
                  pl.BlockSpec((1, T, LANES), lambda b, j, i, bias: (b, 0, j))],
        out_specs=pl.BlockSpec((1, tq, LANES), lambda b, j, i, bias: (b, i, j)),
        scratch_shapes=[pltpu.VMEM((2, tq, LANES), F32), pltpu.VMEM((2, tq, LANES), F32)],
    )
    return pl.pallas_call(
        functools.partial(_sb_prompt_kernel, tq=tq),
        grid_spec=grid_spec,
        out_shape=jax.ShapeDtypeStruct((B, T, D), BF16),
        compiler_params=_cparams(("parallel", "parallel", "arbitrary")),
        name="sb_attn_prompt",
    )(bias.astype(F32) * LOG2E, q, k, v)


def _sb_paged_kernel(pt_ref, q_ref, kn_ref, vn_ref, bias_ref, later_ref, *rest, tnew, page, heads, pps):
    k_refs, v_refs = rest[0:2 * pps:2], rest[1:2 * pps:2]
    o_ref, acc_scr, car_scr = rest[2 * pps:]
    s = pl.program_id(1)

    def segment(ks, vs, mask):
        nk = len(ks) * page
        q = q_ref[0]
        slab = 2 * tnew
        tiles = lambda refs, h: jnp.concatenate([r[0, h] for r in refs], axis=1).astype(BF16)
        zs = []
        for h in range(heads):
            qq = q[(h // 2) * slab:(h // 2 + 1) * slab, :]
            zz = _dot(qq, tiles(ks, h))
            zs.append(zz[(h % 2) * tnew:(h % 2 + 1) * tnew, :])
        z = jnp.concatenate(zs, axis=0) + bias_ref[:, :nk]
        sp = _softplus2(z)
        if mask is not None:
            sp = jnp.where(mask, sp, 0.0)
        tail = _dot(sp.astype(BF16), later_ref[:nk, :nk])
        carry = car_scr[...]
        pa = jnp.exp2(z - sp - tail - jnp.tile(carry, (1, nk // LANES)))
        if mask is not None:
            pa = jnp.where(mask, pa, 0.0)
        pa = pa.astype(BF16)
        for h in range(heads):
            res = _dot_nt(pa[(h // 2) * slab:(h // 2 + 1) * slab, :], tiles(vs, h))
            acc_scr[h * tnew:(h + 1) * tnew, :] += res[(h % 2) * tnew:(h % 2 + 1) * tnew, :]
        car_scr[...] = carry + jnp.broadcast_to(tail[:, 0:1] + sp[:, 0:1], carry.shape)

    @pl.when(s == 0)
    def _():
        acc_scr[...] = jnp.zeros_like(acc_scr)
        car_scr[...] = jnp.zeros_like(car_scr)
        key = lax.broadcasted_iota(jnp.int32, (LANES, page), 1)
        qi = lax.broadcasted_iota(jnp.int32, (LANES, page), 0) % tnew
        segment([kn_ref], [vn_ref], key < qi)

    @pl.when(s > 0)
    def _():
        segment(k_refs, v_refs, None)

    @pl.when(s == pl.num_programs(1) - 1)
    def _():
        o_ref[0] = acc_scr[...]


def _sb_paged(q, k_new, v_new, bias, cache_k, cache_v, page_table):
    B, tnew, D = q.shape
    H = D // HEAD_DIM
    n_pool, page = cache_k.shape[0], cache_k.shape[1]
    npg = page_table.shape[1]
    assert H * tnew == LANES and page == LANES and tnew % SUBLANES == 0
    ck = cache_k.transpose(0, 2, 3, 1)
    cv = cache_v.transpose(0, 2, 3, 1)
    new = lambda z: jnp.pad(z.reshape(B, tnew, H, HEAD_DIM).transpose(0, 2, 3, 1),
                            ((0, 0), (0, 0), (0, 0), (0, page - tnew)))
    qrows = q.reshape(B, tnew, H, HEAD_DIM).transpose(0, 2, 1, 3).reshape(B, LANES, HEAD_DIM)
    pps = next(n for n in (4, 2, 1) if npg % n == 0)
    nk = pps * page
    bias_rows = jnp.broadcast_to(jnp.repeat(bias.astype(F32) * LOG2E, tnew)[:, None], (LANES, nk))
    later = (jnp.arange(nk)[:, None] > jnp.arange(nk)[None, :]).astype(BF16)

    def page_map(slot):
        return lambda b, s, pt: (pt[b * npg + npg - jnp.maximum(s, 1) * pps + slot], 0, 0, 0)

    seq3 = lambda b, s, pt: (b, 0, 0)
    seq4 = lambda b, s, pt: (b, 0, 0, 0)
    const = lambda b, s, pt: (0, 0)
    pg = (1, H, HEAD_DIM, page)
    page_specs, page_args = [], []
    for slot in range(pps):
        page_specs += [pl.BlockSpec(pg, page_map(slot)), pl.BlockSpec(pg, page_map(slot))]
        page_args += [ck, cv]
    grid_spec = pltpu.PrefetchScalarGridSpec(
        num_scalar_prefetch=1,
        grid=(B, npg // pps + 1),
        in_specs=[pl.BlockSpec((1, LANES, HEAD_DIM), seq3),
                  pl.BlockSpec(pg, seq4), pl.BlockSpec(pg, seq4),
                  pl.BlockSpec((LANES, nk), const), pl.BlockSpec((nk, nk), const)] + page_specs,
        out_specs=pl.BlockSpec((1, LANES, HEAD_DIM), seq3),
        scratch_shapes=[pltpu.VMEM((LANES, HEAD_DIM), F32), pltpu.VMEM((LANES, LANES), F32)],
    )
    o = pl.pallas_call(
        functools.partial(_sb_paged_kernel, tnew=tnew, page=page, heads=H, pps=pps),
        grid_spec=grid_spec,
        out_shape=jax.ShapeDtypeStruct((B, LANES, HEAD_DIM), F32),
        compiler_params=_cparams(("parallel", "arbitrary")),
        name="sb_attn_paged",
    )(page_table.reshape(-1).astype(jnp.int32), qrows, new(k_new), new(v_new), bias_rows, later, *page_args)
    return o.reshape(B, H, tnew, HEAD_DIM).transpose(0, 2, 1, 3).reshape(B, tnew, D).astype(BF16)


def _attn_out_router_kernel(o_ref, x_ref, wo_ref, nw_ref, rt_ref, x_o, xn_o, route_o):
    x = x_ref[...] + _dot(o_ref[...], wo_ref[...])
    xn = _rms(x, nw_ref[...])
    x_o[...] = x
    xn_o[...] = xn
    n_exp = rt_ref.shape[1]
    logits = _mm(xn, rt_ref[...])
    lane = lax.broadcasted_iota(jnp.int32, logits.shape, 1).astype(F32)
    v1 = jnp.max(logits, axis=-1, keepdims=True)
    i1 = jnp.min(jnp.where(logits == v1, lane, float(n_exp)), axis=-1, keepdims=True)
    rest = jnp.where(lane == i1, -jnp.inf, logits)
    v2 = jnp.max(rest, axis=-1, keepdims=True)
    i2 = jnp.min(jnp.where(rest == v2, lane, float(n_exp)), axis=-1, keepdims=True)
    e2 = jnp.exp(v2 - v1)
    g1 = 1.0 / (1.0 + e2)
    g2 = e2 * g1
    col = lax.broadcasted_iota(jnp.int32, (x.shape[0], ROUTE_W), 1)
    route_o[...] = (jnp.where(col == 0, i1, 0.0) + jnp.where(col == 1, i2, 0.0)
                    + jnp.where(col == 2, g1, 0.0) + jnp.where(col == 3, g2, 0.0))


def _attn_out_router(o, x, p):
    N, D = x.shape
    tm = min(N, 512)
    E = p["moe_router"].shape[1]
    router = p["moe_router"].astype(F32)
    tile = pl.BlockSpec((tm, D), lambda i: (i, 0))
    return pl.pallas_call(
        _attn_out_router_kernel,
        grid=(N // tm,),
        in_specs=[tile, tile, pl.BlockSpec((D, D), lambda i: (0, 0)), pl.BlockSpec((1, D), lambda i: (0, 0)),
                  pl.BlockSpec((D, E), lambda i: (0, 0))],
        out_specs=[tile, tile, pl.BlockSpec((tm, ROUTE_W), lambda i: (i, 0))],
        out_shape=[jax.ShapeDtypeStruct((N, D), F32), jax.ShapeDtypeStruct((N, D), F32),
                   jax.ShapeDtypeStruct((N, ROUTE_W), F32)],
        compiler_params=_cparams(("parallel",)),
        name="attn_out_router",
    )(o, x, p["sb_w_o"].astype(BF16), p["norm_ffn1"].reshape(1, D).astype(F32), router)


def _moe_plan(route, n_exp, tm, n_rows):
    n = route.shape[0]
    eid = jnp.concatenate([route[:, 0], route[:, 1]]).astype(jnp.int32)
    onehot = (eid[:, None] == jnp.arange(n_exp, dtype=jnp.int32)[None, :]).astype(jnp.int32)
    cum = jnp.cumsum(onehot, axis=0)
    counts = cum[-1]
    rank = jnp.sum(onehot * (cum - 1), axis=1)
    padded = (counts + tm - 1) // tm * tm
    ends = jnp.cumsum(padded)
    dest = (ends - padded)[eid] + rank
    src = jnp.zeros((n_rows,), jnp.int32).at[dest].set(jnp.tile(jnp.arange(n, dtype=jnp.int32), TOP_K))
    tile_start = jnp.arange(n_rows // tm, dtype=jnp.int32) * tm
    tile_exp = jnp.minimum(jnp.searchsorted(ends, tile_start, side="right"), n_exp - 1).astype(jnp.int32)
    active = (tile_start < ends[-1]).astype(jnp.int32)
    return src, dest, tile_exp, active


def _row_copy(src_hbm, row, dst, slot, sem):
    return pltpu.make_async_copy(src_hbm.at[pl.ds(row, 1), :], dst.at[pl.ds(slot, 1), :], sem)


def _moe_gather_kernel(src_ref, x_hbm, o_ref, sem):
    rows = o_ref.shape[0]

    def issue(r, carry):
        _row_copy(x_hbm, src_ref[r // LANES, r % LANES], o_ref, r, sem).start()
        return carry

    lax.fori_loop(0, rows, issue, 0, unroll=8)
    pltpu.make_async_copy(x_hbm.at[pl.ds(0, rows), :], o_ref, sem).wait()


def _moe_gather(xn, src, gt):
    n_rows = src.shape[0]
    D = xn.shape[1]
    return pl.pallas_call(
        _moe_gather_kernel,
        grid=(n_rows // gt,),
        in_specs=[pl.BlockSpec((gt // LANES, LANES), lambda i: (i, 0), memory_space=pltpu.SMEM),
                  pl.BlockSpec(memory_space=pl.ANY)],
        out_specs=pl.BlockSpec((gt, D), lambda i: (i, 0)),
        out_shape=jax.ShapeDtypeStruct((n_rows, D), xn.dtype),
        scratch_shapes=[pltpu.SemaphoreType.DMA(())],
        compiler_params=_cparams(("arbitrary",)),
        name="moe_gather",
    )(src.reshape(n_rows // LANES, LANES), xn)


def _moe_experts_kernel(te_ref, act_ref, xs_ref, wg_ref, wu_ref, wo_ref, o_ref, acc_scr):
    i = pl.program_id(0)
    c = pl.program_id(1)

    @pl.when(c == 0)
    def _():
        acc_scr[...] = jnp.zeros_like(acc_scr)

    @pl.when(act_ref[i] > 0)
    def _():
        xs = xs_ref[...].astype(BF16)
        gate = _dot(xs, wg_ref[0])
        up = _dot(xs, wu_ref[0])
        h = (gate * _sigmoid(gate) * up).astype(BF16)
        acc_scr[...] += _dot(h, wo_ref[0])

    @pl.when(c == pl.num_programs(1) - 1)
    def _():
        o_ref[...] = acc_scr[...]


def _moe_experts(xs, tile_exp, active, w_in, w_out, tm):
    n_rows, D = xs.shape
    F = w_out.shape[1]
    fc = _ffn_chunk(F)
    nfc = F // fc
    wi = w_in.astype(BF16)
    grid_spec = pltpu.PrefetchScalarGridSpec(
        num_scalar_prefetch=2,
        grid=(n_rows // tm, nfc),
        in_specs=[pl.BlockSpec((tm, D), lambda i, c, te, act: (i, 0)),
                  pl.BlockSpec((1, D, fc), lambda i, c, te, act: (te[i], 0, c * act[i])),
                  pl.BlockSpec((1, D, fc), lambda i, c, te, act: (te[i], 0, nfc + c * act[i])),
                  pl.BlockSpec((1, fc, D), lambda i, c, te, act: (te[i], c * act[i], 0))],
        out_specs=pl.BlockSpec((tm, D), lambda i, c, te, act: (i, 0)),
        scratch_shapes=[pltpu.VMEM((tm, D), F32)],
    )
    return pl.pallas_call(
        _moe_experts_kernel,
        grid_spec=grid_spec,
        out_shape=jax.ShapeDtypeStruct((n_rows, D), F32),
        compiler_params=_cparams(("parallel", "arbitrary")),
        name="moe_experts",
    )(tile_exp, active, xs, wi, wi, w_out.astype(BF16))


def _moe_combine_kernel(pos_ref, x_ref, route_ref, ys_hbm, o_ref, buf, sem):
    tm = x_ref.shape[0]

    def issue(t, carry):
        for k in range(TOP_K):
            _row_copy(ys_hbm, pos_ref[k, t // LANES, t % LANES], buf.at[k], t, sem).start()
        return carry

    lax.fori_loop(0, tm, issue, 0, unroll=4)
    for k in range(TOP_K):
        pltpu.make_async_copy(ys_hbm.at[pl.ds(0, tm), :], buf.at[k], sem).wait()
    route = route_ref[...]
    o_ref[...] = x_ref[...] + route[:, 2:3] * buf[0] + route[:, 3:4] * buf[1]


def _moe_combine(x, route, ys, dest, ct):
    N, D = x.shape
    pos = dest.reshape(TOP_K, N // LANES, LANES)
    return pl.pallas_call(
        _moe_combine_kernel,
        grid=(N // ct,),
        in_specs=[pl.BlockSpec((TOP_K, ct // LANES, LANES), lambda i: (0, i, 0), memory_space=pltpu.SMEM),
                  pl.BlockSpec((ct, D), lambda i: (i, 0)),
                  pl.BlockSpec((ct, ROUTE_W), lambda i: (i, 0)),
                  pl.BlockSpec(memory_space=pl.ANY)],
        out_specs=pl.BlockSpec((ct, D), lambda i: (i, 0)),
        out_shape=jax.ShapeDtypeStruct((N, D), F32),
        scratch_shapes=[pltpu.VMEM((TOP_K, ct, D), F32), pltpu.SemaphoreType.DMA(())],
        compiler_params=_cparams(("arbitrary",)),
        name="moe_combine",
    )(pos, x, route, ys)


def _moe(x, xn, route, w_in, w_out):
    N, D = x.shape
    E = w_out.shape[0]
    tm = min(N, 512)
    gt = 2 * SUBLANES * LANES
    assert gt % tm == 0
    n_rows = -(-(TOP_K * N + E * tm) // gt) * gt
    src, dest, tile_exp, active = _moe_plan(route, E, tm, n_rows)
    xs = _moe_gather(xn, src, gt)
    ys = _moe_experts(xs, tile_exp, active, w_in, w_out, tm)
    return _moe_combine(x, route, ys, dest, min(N, SUBLANES * LANES))


def _decoder_group(x, shift0, wkv0, past, p):
    B, T, D = x.shape
    N = B * T
    r, lw, k, v, a, b, g, bonus, last = _rwkv_pre(x, shift0, p)
    y, s_fin = _rwkv_scan(r, lw, k, v, a, b, wkv0)
    x1 = _rwkv_post(y.reshape(N, D), bonus.reshape(N, D), g.reshape(N, D), x.reshape(N, D), p)
    x2 = _ffn(x1, p["norm_ffn0"], p["ffn_w_in"], p["ffn_w_out"])
    k_sh, v_sh, kb, vb, qb = _kvq(x2, p)
    sh3 = lambda z: z.reshape(B, T, D)
    if past is None:
        o = _sb_prompt(sh3(qb), sh3(kb), sh3(vb), p["sb_bias"])
    else:
        o = _sb_paged(sh3(qb), sh3(k_sh), sh3(v_sh), p["sb_bias"], *past)
    x3, xn3, route = _attn_out_router(o.reshape(N, D), x2, p)
    x4 = _moe(x3, xn3, route, p["moe_w_in"], p["moe_w_out"])
    H = D // HEAD_DIM
    return (x4.reshape(B, T, D), last.reshape(1, B, D), s_fin[None], k_sh.reshape(B, T, H, HEAD_DIM),
            v_sh.reshape(B, T, H, HEAD_DIM))


def kernel(x_prompt, x_sample, cache_k, cache_v, state_wkv, state_shift, page_table, norm_mix, norm_ffn, rw_mix, rw_w_r, rw_w_k, rw_w_v, rw_w_o, rw_w0, rw_w1, rw_w2, rw_a0, rw_a1, rw_a2, rw_g1, rw_g2, rw_k_k, rw_k_a, rw_r_k, rw_ln_w, rw_ln_b, kv_norm, kv_w, kv_k_norm, sb_w_q, sb_q_norm, sb_bias, sb_w_o, ffn_w_in, ffn_w_out, moe_router, moe_w_in, moe_w_out):
    assert norm_mix.shape[0] == 2 and state_wkv.shape[0] == 1, "one RWKV layer followed by one attention layer"
    p = dict(norm_mix0=norm_mix[0], norm_mix1=norm_mix[1], norm_ffn0=norm_ffn[0], norm_ffn1=norm_ffn[1],
             rw_mix=rw_mix[0], rw_w_r=rw_w_r[0], rw_w_k=rw_w_k[0], rw_w_v=rw_w_v[0], rw_w_o=rw_w_o[0],
             rw_w0=rw_w0[0], rw_w1=rw_w1[0], rw_w2=rw_w2[0], rw_a0=rw_a0[0], rw_a1=rw_a1[0], rw_a2=rw_a2[0],
             rw_g1=rw_g1[0], rw_g2=rw_g2[0], rw_k_k=rw_k_k[0], rw_k_a=rw_k_a[0], rw_r_k=rw_r_k[0].reshape(-1),
             rw_ln_w=rw_ln_w[0], rw_ln_b=rw_ln_b[0], kv_norm=kv_norm, kv_w=kv_w, kv_k_norm=kv_k_norm,
             sb_w_q=sb_w_q[0], sb_q_norm=sb_q_norm[0], sb_bias=sb_bias[0], sb_w_o=sb_w_o[0],
             ffn_w_in=ffn_w_in[0], ffn_w_out=ffn_w_out[0], moe_router=moe_router[0], moe_w_in=moe_w_in[0],
             moe_w_out=moe_w_out[0])
    bp, _, d = x_prompt.shape
    h = d // HEAD_DIM
    y_p, shift_p, wkv_p, k_p, v_p = _decoder_group(
        x_prompt, jnp.zeros((bp, d), x_prompt.dtype), jnp.zeros((bp, h, HEAD_DIM, HEAD_DIM), x_prompt.dtype), None, p)
    y_s, shift_s, wkv_s, k_s, v_s = _decoder_group(
        x_sample, state_shift[0], state_wkv[0], (cache_k, cache_v, page_table), p)
    return (y_p, y_s, wkv_p, shift_p, k_p, v_p, wkv_s, shift_s, k_s, v_s)
```

```python
import functools
import math

import jax
import jax.numpy as jnp
from jax import lax
from jax.experimental import pallas as pl
from jax.experimental.pallas import tpu as pltpu

F32 = jnp.float32
BF16 = jnp.bfloat16

HEAD_DIM = 64
LANES = 128
SUBLANES = 8
RMS_EPS = 1e-6
RW_GN_EPS = 64e-5
TOP_K = 2
ROUTE_W = 8
LOG2E = math.log2(math.e)
WALK = 4
SCAN_CHUNK = 64
VMEM_LIMIT = 52 * 1024 * 1024


def _cparams(sem):
    return pltpu.CompilerParams(dimension_semantics=sem, vmem_limit_bytes=VMEM_LIMIT)


def _dot(a, b):
    return jnp.dot(a, b, preferred_element_type=F32)


def _dot_nt(a, b):
    return lax.dot_general(a, b, (((1,), (1,)), ((), ())), preferred_element_type=F32)


def _dot_tn(a, b):
    return lax.dot_general(a, b, (((0,), (0,)), ((), ())), preferred_element_type=F32)


def _split(x, n):
    parts = []
    rem = x
    for i in range(n):
        p = rem.astype(BF16)
        parts.append(p)
        if i + 1 < n:
            rem = rem - p.astype(F32)
    return parts


def _mm(a, b, kind="nn", na=3, nb=3):
    f = {"nn": _dot, "nt": _dot_nt, "tn": _dot_tn}[kind]
    ap = _split(a, na) if a.dtype != BF16 else [a]
    bp = _split(b, nb) if b.dtype != BF16 else [b]
    keep = max(len(ap), len(bp))
    out = None
    for i, x in enumerate(ap):
        for j, y in enumerate(bp):
            if i + j < keep:
                t = f(x, y)
                out = t if out is None else out + t
    return out


def _rms(x, gain):
    return x * lax.rsqrt(jnp.mean(x * x, axis=-1, keepdims=True) + RMS_EPS) * gain


def _softplus(z):
    return jnp.maximum(z, 0.0) + jnp.log(1.0 + jnp.exp(-jnp.abs(z)))


def _softplus2(z):
    neg_abs = lax.bitcast_convert_type(lax.bitcast_convert_type(z, jnp.uint32) | jnp.uint32(0x80000000), F32)
    return jnp.maximum(z, 0.0) + jnp.log(1.0 + jnp.exp2(neg_abs)) * LOG2E


def _sigmoid(z):
    return 1.0 / (1.0 + jnp.exp(-z))


def _head_sum(x, e, et):
    s = _mm(x, e, na=2)
    return _mm(s, et, na=2)


def _head_onehot(d):
    h = d // HEAD_DIM
    e = (jnp.arange(d)[:, None] // HEAD_DIM == jnp.arange(LANES)[None, :]).astype(BF16)
    del h
    return e, e.T


def _rwkv_pre_kernel(x_ref, xp_ref, sh_ref, nm_ref, mix_ref, wr_ref, wk_ref, wv_ref, w0_ref, w1_ref,
                     w2_ref, a0_ref, a1_ref, a2_ref, g1_ref, g2_ref, kk_ref, ka_ref, rk_ref, e_ref,
                     et_ref, r_o, lw_o, k_o, v_o, a_o, b_o, g_o, bonus_o, last_o):
    t = pl.program_id(1)
    gain = nm_ref[...]
    x = x_ref[0]
    tt = x.shape[0]
    xn = _rms(x, gain)
    prev_tile_last = _rms(xp_ref[0][SUBLANES - 1:SUBLANES, :], gain)
    first = jnp.where(t == 0, sh_ref[0], prev_tile_last)
    row = lax.broadcasted_iota(jnp.int32, xn.shape, 0)
    prev = jnp.where(row == 0, first, pltpu.roll(xn, 1, 0))
    dx = prev - xn

    def mixed(i):
        return (xn + dx * mix_ref[i:i + 1, :]).astype(BF16)

    r = _dot(mixed(0), wr_ref[...])
    lw_raw = w0_ref[...] + _dot(jnp.tanh(_dot(mixed(1), w1_ref[...])).astype(BF16), w2_ref[...])
    k = _dot(mixed(2), wk_ref[...])
    v = _dot(mixed(3), wv_ref[...])
    asig = _sigmoid(a0_ref[...] + _dot(_dot(mixed(4), a1_ref[...]).astype(BF16), a2_ref[...]))
    g = _dot(_sigmoid(_dot(mixed(5), g1_ref[...])).astype(BF16), g2_ref[...])

    log_w = -_softplus(-lw_raw) - 0.5
    lw_o[0] = -jnp.exp(log_w)
    e, et = e_ref[...], et_ref[...]
    kkr = k * kk_ref[...]
    nrm = jnp.maximum(jnp.sqrt(_head_sum(kkr * kkr, e, et)), 1e-12)
    kk = kkr * (1.0 / nrm)
    kmod = k * (1.0 + (asig - 1.0) * ka_ref[...])
    r_o[0] = r
    k_o[0] = kmod
    v_o[0] = v
    a_o[0] = -kk
    b_o[0] = kk * asig
    g_o[0] = g
    bonus_o[0] = _head_sum(r * kmod * rk_ref[...], e, et) * v
    last_o[0] = xn[tt - 1:tt, :]


def _rwkv_pre(x, shift0, p):
    B, T, D = x.shape
    tt = min(T, 256)
    nt = T // tt
    e, et = _head_onehot(D)
    row = lambda a: a.reshape(1, -1).astype(F32)
    bf = lambda a: a.astype(BF16)
    consts = [row(p["norm_mix0"]), p["rw_mix"].astype(F32), bf(p["rw_w_r"]), bf(p["rw_w_k"]), bf(p["rw_w_v"]),
              row(p["rw_w0"]), bf(p["rw_w1"]), bf(p["rw_w2"]), row(p["rw_a0"]), bf(p["rw_a1"]), bf(p["rw_a2"]),
              bf(p["rw_g1"]), bf(p["rw_g2"]), row(p["rw_k_k"]), row(p["rw_k_a"]), row(p["rw_r_k"]), e, et]
    const_specs = [pl.BlockSpec(c.shape, lambda b, t, n=c.ndim: (0,) * n) for c in consts]
    tile = pl.BlockSpec((1, tt, D), lambda b, t: (b, t, 0))
    prev8 = pl.BlockSpec((1, SUBLANES, D), lambda b, t: (b, jnp.maximum(t * (tt // SUBLANES) - 1, 0), 0))
    one = pl.BlockSpec((1, 1, D), lambda b, t: (b, 0, 0))
    big = jax.ShapeDtypeStruct((B, T, D), F32)
    outs = pl.pallas_call(
        _rwkv_pre_kernel,
        grid=(B, nt),
        in_specs=[tile, prev8, one] + const_specs,
        out_specs=[tile] * 8 + [one],
        out_shape=[big] * 8 + [jax.ShapeDtypeStruct((B, 1, D), F32)],
        compiler_params=_cparams(("parallel", "arbitrary")),
        name="rwkv_pre",
    )(x, x, shift0.reshape(B, 1, D), *consts)
    return outs


def _scan_kernel(r_ref, lw_ref, k_ref, v_ref, a_ref, b_ref, s0_ref, y_ref, sT_ref, s_scr, *, nchunk, npair):
    C = SCAN_CHUNK
    c = pl.program_id(2)

    @pl.when(c == 0)
    def _():
        s_scr[...] = s0_ref[0]

    row2 = lax.broadcasted_iota(jnp.int32, (2 * C, LANES), 0)
    lane2 = lax.broadcasted_iota(jnp.int32, (2 * C, LANES), 1)
    own = (lane2 // HEAD_DIM) == (row2 // C)
    ri = lax.broadcasted_iota(jnp.int32, (2 * C, 2 * C), 0)
    ci = lax.broadcasted_iota(jnp.int32, (2 * C, 2 * C), 1)
    strict = ci < ri
    incl = ci <= ri
    eye = (ci == ri).astype(F32)
    tri = (lax.broadcasted_iota(jnp.int32, (C, C), 1) <= lax.broadcasted_iota(jnp.int32, (C, C), 0)).astype(BF16)

    def stack(x):
        return jnp.where(own, jnp.concatenate([x, x], axis=0), 0.0).astype(BF16)

    def mm(x, y, kind="nn"):
        return _mm(x.astype(BF16), y.astype(BF16), kind)

    chains = [(p, g) for p in range(npair) for g in range(nchunk)]
    w = {}
    for ch in chains:
        p, g = ch
        sl, ln = pl.ds(g * C, C), pl.ds(p * LANES, LANES)
        r, lw, k, v, a, b = (ref[0, sl, ln] for ref in (r_ref, lw_ref, k_ref, v_ref, a_ref, b_ref))
        gc = _mm(tri, lw, nb=2)
        gt = gc[C - 1:C, :]
        eneg = jnp.exp(-gc)
        efar = jnp.exp(gt - gc)
        w[ch] = dict(rt=stack(r * jnp.exp(gc)), at=stack(a * jnp.exp(gc - lw)), bt=stack(b * eneg),
                     kt=stack(k * eneg), bw=stack(b * efar), kw=stack(k * efar), vs=stack(v), dec=jnp.exp(gt))
    for ch in chains:
        c_ = w[ch]
        G = mm(jnp.concatenate([c_["at"], c_["rt"]], axis=0), jnp.concatenate([c_["bt"], c_["kt"]], axis=0), "nt")
        c_["L"] = jnp.where(strict, G[:2 * C, :2 * C], 0.0)
        c_["Lak"] = jnp.where(strict, G[:2 * C, 2 * C:], 0.0)
        c_["RB"] = jnp.where(incl, G[2 * C:, :2 * C], 0.0).astype(BF16)
        c_["RK"] = jnp.where(incl, G[2 * C:, 2 * C:], 0.0)
    for ch in chains:
        c_ = w[ch]
        c_["Tinv"] = eye + c_["L"]
        c_["A"] = mm(c_["L"], c_["L"])
        c_["LV"] = mm(c_["Lak"], c_["vs"]).astype(BF16)
        c_["Y0"] = mm(c_["RK"], c_["vs"])
        c_["Z0"] = mm(c_["vs"], c_["kw"], "tn")
    for _ in range(int(math.log2(C)) - 2):
        for ch in chains:
            c_ = w[ch]
            both = mm(c_["A"], jnp.concatenate([c_["A"], c_["Tinv"]], axis=1))
            c_["A"] = both[:, :2 * C]
            c_["Tinv"] = c_["Tinv"] + both[:, 2 * C:]
    for ch in chains:
        c_ = w[ch]
        c_["Tinv"] = c_["Tinv"] + mm(c_["A"], c_["Tinv"])
    for ch in chains:
        c_ = w[ch]
        PQ = mm(c_["Tinv"], jnp.concatenate([c_["at"], c_["LV"]], axis=1))
        c_["P"] = PQ[:, :LANES].astype(BF16)
        c_["QT"] = PQ[:, LANES:].T
    S = [s_scr[p] for p in range(npair)]
    for g in range(nchunk):
        for p in range(npair):
            c_ = w[(p, g)]
            c_["S"] = S[p].astype(BF16)
            UT = mm(c_["S"], c_["P"], "nt") + c_["QT"]
            c_["UT"] = UT
            S[p] = S[p] * c_["dec"] + mm(UT, c_["bw"]) + c_["Z0"]
    for p in range(npair):
        s_scr[p] = S[p]
    for ch in chains:
        p, g = ch
        c_ = w[ch]
        Y = mm(c_["rt"], c_["S"], "nt") + mm(c_["RB"], c_["UT"].T) + c_["Y0"]
        y_ref[0, pl.ds(g * C, C), pl.ds(p * LANES, LANES)] = Y[:C] + Y[C:]

    @pl.when(c == pl.num_programs(2) - 1)
    def _():
        sT_ref[0] = s_scr[...]


def _rwkv_scan(r, lw, k, v, a, b, s0):
    B, T, D = r.shape
    H = D // HEAD_DIM
    NP = D // LANES
    C = SCAN_CHUNK
    Tp = -(-T // C) * C
    if Tp != T:
        pad = lambda z: jnp.pad(z, ((0, 0), (0, Tp - T), (0, 0)))
        r, lw, k, v, a, b = (pad(z) for z in (r, lw, k, v, a, b))
    nchunk = next(n for n in (8, 4, 1) if Tp % (n * C) == 0)
    npair = next(n for n in (8, 4, 2, 1) if NP % n == 0 and n * nchunk <= 16)
    tb = nchunk * C
    s0p = s0.astype(F32).reshape(B, NP, 2, HEAD_DIM, HEAD_DIM)
    z = jnp.zeros_like(s0p[:, :, 0])
    s0bd = jnp.concatenate([jnp.concatenate([s0p[:, :, 0], z], axis=-1),
                            jnp.concatenate([z, s0p[:, :, 1]], axis=-1)], axis=-2)
    tile = pl.BlockSpec((1, tb, npair * LANES), lambda bb, j, c: (bb, c, j))
    st = pl.BlockSpec((1, npair, LANES, LANES), lambda bb, j, c: (bb, j, 0, 0))
    y, sT = pl.pallas_call(
        functools.partial(_scan_kernel, nchunk=nchunk, npair=npair),
        grid=(B, NP // npair, Tp // tb),
        in_specs=[tile] * 6 + [st],
        out_specs=[tile, st],
        out_shape=[jax.ShapeDtypeStruct((B, Tp, D), F32), jax.ShapeDtypeStruct((B, NP, LANES, LANES), F32)],
        scratch_shapes=[pltpu.VMEM((npair, LANES, LANES), F32)],
        compiler_params=_cparams(("parallel", "parallel", "arbitrary")),
        name="rwkv_scan",
    )(r, lw, k, v, a, b, s0bd)
    s_fin = jnp.stack([sT[:, :, :HEAD_DIM, :HEAD_DIM], sT[:, :, HEAD_DIM:, HEAD_DIM:]], axis=2)
    return y[:, :T], s_fin.reshape(B, H, HEAD_DIM, HEAD_DIM)


def _rwkv_post_kernel(y_ref, bonus_ref, g_ref, x_ref, lnw_ref, lnb_ref, wo_ref, e_ref, et_ref, o_ref):
    e, et = e_ref[...], et_ref[...]
    y = y_ref[...]
    mu = _head_sum(y, e, et) * (1.0 / HEAD_DIM)
    d = y - mu
    var = _head_sum(d * d, e, et) * (1.0 / HEAD_DIM)
    yn = d * lax.rsqrt(var + RW_GN_EPS) * lnw_ref[...] + lnb_ref[...]
    z = ((yn + bonus_ref[...]) * g_ref[...]).astype(BF16)
    o_ref[...] = x_ref[...] + _dot(z, wo_ref[...])


def _rwkv_post(y, bonus, g, x, p):
    N, D = x.shape
    tm = min(N, 512)
    e, et = _head_onehot(D)
    consts = [p["rw_ln_w"].reshape(1, D).astype(F32), p["rw_ln_b"].reshape(1, D).astype(F32),
              p["rw_w_o"].astype(BF16), e, et]
    tile = pl.BlockSpec((tm, D), lambda i: (i, 0))
    return pl.pallas_call(
        _rwkv_post_kernel,
        grid=(N // tm,),
        in_specs=[tile] * 4 + [pl.BlockSpec(c.shape, lambda i: (0, 0)) for c in consts],
        out_specs=tile,
        out_shape=jax.ShapeDtypeStruct((N, D), F32),
        compiler_params=_cparams(("parallel",)),
        name="rwkv_post",
    )(y, bonus, g, x, *consts)


def _ffn_kernel(x_ref, nw_ref, wg_ref, wu_ref, wo_ref, o_ref, xn_scr, acc_scr):
    c = pl.program_id(1)

    @pl.when(c == 0)
    def _():
        xn_scr[...] = _rms(x_ref[...], nw_ref[...]).astype(BF16)
        acc_scr[...] = jnp.zeros_like(acc_scr)

    xn = xn_scr[...]
    gate = _dot(xn, wg_ref[...])
    up = _dot(xn, wu_ref[...])
    h = (gate * _sigmoid(gate) * up).astype(BF16)
    acc_scr[...] += _dot(h, wo_ref[...])

    @pl.when(c == pl.num_programs(1) - 1)
    def _():
        o_ref[...] = x_ref[...] + acc_scr[...]


def _ffn_chunk(f):
    for fc in (1408, 896, 512, 256, 128):
        if f % fc == 0:
            return fc
    return f


def _ffn(x, norm_w, w_in, w_out):
    N, D = x.shape
    F = w_out.shape[0]
    tm = min(N, 512)
    fc = _ffn_chunk(F)
    nfc = F // fc
    return pl.pallas_call(
        _ffn_kernel,
        grid=(N // tm, nfc),
        in_specs=[pl.BlockSpec((tm, D), lambda i, c: (i, 0)),
                  pl.BlockSpec((1, D), lambda i, c: (0, 0)),
                  pl.BlockSpec((D, fc), lambda i, c: (0, c)),
                  pl.BlockSpec((D, fc), lambda i, c: (0, nfc + c)),
                  pl.BlockSpec((fc, D), lambda i, c: (c, 0))],
        out_specs=pl.BlockSpec((tm, D), lambda i, c: (i, 0)),
        out_shape=jax.ShapeDtypeStruct((N, D), F32),
        scratch_shapes=[pltpu.VMEM((tm, D), BF16), pltpu.VMEM((tm, D), F32)],
        compiler_params=_cparams(("parallel", "arbitrary")),
        name="ffn",
    )(x, norm_w.reshape(1, D).astype(F32), w_in.astype(BF16), w_in.astype(BF16), w_out.astype(BF16))


def _kvq_kernel(x_ref, kvn_ref, qn_ref, wkv_ref, wq_ref, kg_ref, qg_ref, e_ref, et_ref,
                k_o, v_o, kb_o, vb_o, qb_o):
    e, et = e_ref[...], et_ref[...]
    x = x_ref[...]
    D = x.shape[1]
    xhat = x * lax.rsqrt(jnp.mean(x * x, axis=-1, keepdims=True) + RMS_EPS)
    kv = _dot((xhat * kvn_ref[...]).astype(BF16), wkv_ref[...])
    kraw = kv[:, :D]
    v = kv[:, D:]
    k = kraw * lax.rsqrt(_head_sum(kraw * kraw, e, et) * (1.0 / HEAD_DIM) + RMS_EPS) * kg_ref[...]
    q = _dot((xhat * qn_ref[...]).astype(BF16), wq_ref[...])
    q = q * lax.rsqrt(_head_sum(q * q, e, et) * (1.0 / HEAD_DIM) + RMS_EPS) * qg_ref[...]
    k_o[...] = k
    v_o[...] = v
    kb_o[...] = k.astype(BF16)
    vb_o[...] = v.astype(BF16)
    qb_o[...] = (q * (LOG2E / math.sqrt(HEAD_DIM))).astype(BF16)


def _kvq(x, p):
    N, D = x.shape
    H = D // HEAD_DIM
    tm = min(N, 512)
    e, et = _head_onehot(D)
    row = lambda a: a.reshape(1, D).astype(F32)
    consts = [row(p["kv_norm"]), row(p["norm_mix1"]), p["kv_w"].astype(BF16), p["sb_w_q"].astype(BF16),
              row(jnp.tile(p["kv_k_norm"], H)), row(jnp.tile(p["sb_q_norm"], H)), e, et]
    tile = pl.BlockSpec((tm, D), lambda i: (i, 0))
    f = jax.ShapeDtypeStruct((N, D), F32)
    h = jax.ShapeDtypeStruct((N, D), BF16)
    return pl.pallas_call(
        _kvq_kernel,
        grid=(N // tm,),
        in_specs=[tile] + [pl.BlockSpec(c.shape, lambda i: (0, 0)) for c in consts],
        out_specs=[tile] * 5,
        out_shape=[f, f, h, h, h],
        compiler_params=_cparams(("parallel",)),
        name="kvq_proj",
    )(x, *consts)


def _sb_prompt_kernel(bias_ref, q_ref, k_ref, v_ref, o_ref, acc_scr, car_scr, *, tq):
    j = pl.program_id(1)
    i = pl.program_id(2)
    q = q_ref[0]
    lane = lax.broadcasted_iota(jnp.int32, q.shape, 1)
    qh = [jnp.where(lane < HEAD_DIM, q, jnp.zeros_like(q)), jnp.where(lane >= HEAD_DIM, q, jnp.zeros_like(q))]
    ri = lax.broadcasted_iota(jnp.int32, (tq, tq), 0)
    ci = lax.broadcasted_iota(jnp.int32, (tq, tq), 1)
    later = (ri > ci).astype(BF16)
    causal = ci < ri
    rep = tq // LANES

    def kv(kb):
        start = pl.multiple_of(kb * tq, tq)
        return k_ref[0, pl.ds(start, tq), :], v_ref[0, pl.ds(start, tq), :]

    def logits(h, kblk):
        return _dot_nt(qh[h], kblk) + bias_ref[2 * j + h]

    def sums(z, masked):
        sp = _softplus2(z)
        if masked:
            sp = jnp.where(causal, sp, 0.0)
        tail = _dot(sp.astype(BF16), later)
        total = jnp.broadcast_to(tail[:, 0:1] + sp[:, 0:1], (tq, LANES))
        return z - sp - tail, total

    def weights(part, carry, vblk, masked):
        pa = jnp.exp2(part - jnp.tile(carry, (1, rep)))
        if masked:
            pa = jnp.where(causal, pa, 0.0)
        return _dot(pa.astype(BF16), vblk)

    def walk(first, n, diagonal):
        blocks = [kv(first - t) for t in range(n)]
        masks = [diagonal and t == 0 for t in range(n)]
        zs = [[logits(h, kb) for kb, _ in blocks] for h in range(2)]
        ps = [[sums(z, m) for z, m in zip(zs[h], masks)] for h in range(2)]
        for h in range(2):
            carry = car_scr[h]
            out = None
            for (part, total), (_, vb), m in zip(ps[h], blocks, masks):
                w = weights(part, carry, vb, m)
                out = w if out is None else out + w
                carry = carry + total
            acc_scr[h] += out
            car_scr[h] = carry

    acc_scr[...] = jnp.zeros_like(acc_scr)
    car_scr[...] = jnp.zeros_like(car_scr)
    head = (i + 1) % WALK
    for n in range(1, WALK + 1):
        @pl.when(head == n % WALK)
        def _(n=n):
            walk(i, n, True)
    first_rest = i - jnp.where(head == 0, WALK, head)

    def group(s, _):
        walk(first_rest - WALK * s, WALK, False)
        return 0

    lax.fori_loop(0, (first_rest + 1) // WALK, group, 0)

    o_ref[0] = jnp.where(lane < HEAD_DIM, acc_scr[0], acc_scr[1]).astype(o_ref.dtype)


def _sb_prompt(q, k, v, bias):
    B, T, D = q.shape
    NP = D // LANES
    tq = min(T, 256)
    grid_spec = pltpu.PrefetchScalarGridSpec(
        num_scalar_prefetch=1,
        grid=(B, NP, T // tq),
        in_specs=[pl.BlockSpec((1, tq, LANES), lambda b, j, i, bias: (b, i, j)),
                  pl.BlockSpec((1, T, LANES), lambda b, j, i, bias: (b, 0, j)),
                  pl.BlockSpec((1, T, LANES), lambda b, j, i, bias: (b, 0, j))],
        out_specs=pl.BlockSpec((1, tq, LANES), lambda b, j, i, bias: (b, i, j)),
        scratch_shapes=[pltpu.VMEM((2, tq, LANES), F32), pltpu.VMEM((2, tq, LANES), F32)],
    )
    return pl.pallas_call(
        functools.partial(_sb_prompt_kernel, tq=tq),
        grid_spec=grid_spec,
        out_shape=jax.ShapeDtypeStruct((B, T, D), BF16),
        compiler_params=_cparams(("parallel", "parallel", "arbitrary")),
        name="sb_attn_prompt",
    )(bias.astype(F32) * LOG2E, q, k, v)


def _sb_paged_kernel(pt_ref, q_ref, kn_ref, vn_ref, bias_ref, later_ref, *rest, tnew, page, heads, pps):
    k_refs, v_refs = rest[0:2 * pps:2], rest[1:2 * pps:2]
    o_ref, acc_scr, car_scr = rest[2 * pps:]
    s = pl.program_id(1)

    def segment(ks, vs, mask):
        nk = len(ks) * page
        q = q_ref[0]
        slab = 2 * tnew
        tiles = lambda refs, h: jnp.concatenate([r[0, h] for r in refs], axis=1).astype(BF16)
        zs = []
        for h in range(heads):
            qq = q[(h // 2) * slab:(h // 2 + 1) * slab, :]
            zz = _dot(qq, tiles(ks, h))
            zs.append(zz[(h % 2) * tnew:(h % 2 + 1) * tnew, :])
        z = jnp.concatenate(zs, axis=0) + bias_ref[:, :nk]
        sp = _softplus2(z)
        if mask is not None:
            sp = jnp.where(mask, sp, 0.0)
        tail = _dot(sp.astype(BF16), later_ref[:nk, :nk])
        carry = car_scr[...]
        pa = jnp.exp2(z - sp - tail - jnp.tile(carry, (1, nk // LANES)))
        if mask is not None:
            pa = jnp.where(mask, pa, 0.0)
        pa = pa.astype(BF16)
        for h in range(heads):
            res = _dot_nt(pa[(h // 2) * slab:(h // 2 + 1) * slab, :], tiles(vs, h))
            acc_scr[h * tnew:(h + 1) * tnew, :] += res[(h % 2) * tnew:(h % 2 + 1) * tnew, :]
        car_scr[...] = carry + jnp.broadcast_to(tail[:, 0:1] + sp[:, 0:1], carry.shape)

    @pl.when(s == 0)
    def _():
        acc_scr[...] = jnp.zeros_like(acc_scr)
        car_scr[...] = jnp.zeros_like(car_scr)
        key = lax.broadcasted_iota(jnp.int32, (LANES, page), 1)
        qi = lax.broadcasted_iota(jnp.int32, (LANES, page), 0) % tnew
        segment([kn_ref], [vn_ref], key < qi)

    @pl.when(s > 0)
    def _():
        segment(k_refs, v_refs, None)

    @pl.when(s == pl.num_programs(1) - 1)
    def _():
        o_ref[0] = acc_scr[...]


def _sb_paged(q, k_new, v_new, bias, cache_k, cache_v, page_table):
    B, tnew, D = q.shape
    H = D // HEAD_DIM
    n_pool, page = cache_k.shape[0], cache_k.shape[1]
    npg = page_table.shape[1]
    assert H * tnew == LANES and page == LANES and tnew % SUBLANES == 0
    ck = cache_k.transpose(0, 2, 3, 1)
    cv = cache_v.transpose(0, 2, 3, 1)
    new = lambda z: jnp.pad(z.reshape(B, tnew, H, HEAD_DIM).transpose(0, 2, 3, 1),
                            ((0, 0), (0, 0), (0, 0), (0, page - tnew)))
    qrows = q.reshape(B, tnew, H, HEAD_DIM).transpose(0, 2, 1, 3).reshape(B, LANES, HEAD_DIM)
    pps = next(n for n in (4, 2, 1) if npg % n == 0)
    nk = pps * page
    bias_rows = jnp.broadcast_to(jnp.repeat(bias.astype(F32) * LOG2E, tnew)[:, None], (LANES, nk))
    later = (jnp.arange(nk)[:, None] > jnp.arange(nk)[None, :]).astype(BF16)

    def page_map(slot):
        return lambda b, s, pt: (pt[b * npg + npg - jnp.maximum(s, 1) * pps + slot], 0, 0, 0)

    seq3 = lambda b, s, pt: (b, 0, 0)
    seq4 = lambda b, s, pt: (b, 0, 0, 0)
    const = lambda b, s, pt: (0, 0)
    pg = (1, H, HEAD_DIM, page)
    page_specs, page_args = [], []
    for slot in range(pps):
        page_specs += [pl.BlockSpec(pg, page_map(slot)), pl.BlockSpec(pg, page_map(slot))]
        page_args += [ck, cv]
    grid_spec = pltpu.PrefetchScalarGridSpec(
        num_scalar_prefetch=1,
        grid=(B, npg // pps + 1),
        in_specs=[pl.BlockSpec((1, LANES, HEAD_DIM), seq3),
                  pl.BlockSpec(pg, seq4), pl.BlockSpec(pg, seq4),
                  pl.BlockSpec((LANES, nk), const), pl.BlockSpec((nk, nk), const)] + page_specs,
        out_specs=pl.BlockSpec((1, LANES, HEAD_DIM), seq3),
        scratch_shapes=[pltpu.VMEM((LANES, HEAD_DIM), F32), pltpu.VMEM((LANES, LANES), F32)],
    )
    o = pl.pallas_call(
        functools.partial(_sb_paged_kernel, tnew=tnew, page=page, heads=H, pps=pps),
        grid_spec=grid_spec,
        out_shape=jax.ShapeDtypeStruct((B, LANES, HEAD_DIM), F32),
        compiler_params=_cparams(("parallel", "arbitrary")),
        name="sb_attn_paged",
    )(page_table.reshape(-1).astype(jnp.int32), qrows, new(k_new), new(v_new), bias_rows, later, *page_args)
    return o.reshape(B, H, tnew, HEAD_DIM).transpose(0, 2, 1, 3).reshape(B, tnew, D).astype(BF16)


def _attn_out_router_kernel(o_ref, x_ref, wo_ref, nw_ref, rt_ref, x_o, xn_o, route_o):
    x = x_ref[...] + _dot(o_ref[...], wo_ref[...])
    xn = _rms(x, nw_ref[...])
    x_o[...] = x
    xn_o[...] = xn
    n_exp = rt_ref.shape[1]
    logits = _mm(xn, rt_ref[...])
    lane = lax.broadcasted_iota(jnp.int32, logits.shape, 1).astype(F32)
    v1 = jnp.max(logits, axis=-1, keepdims=True)
    i1 = jnp.min(jnp.where(logits == v1, lane, float(n_exp)), axis=-1, keepdims=True)
    rest = jnp.where(lane == i1, -jnp.inf, logits)
    v2 = jnp.max(rest, axis=-1, keepdims=True)
    i2 = jnp.min(jnp.where(rest == v2, lane, float(n_exp)), axis=-1, keepdims=True)
    e2 = jnp.exp(v2 - v1)
    g1 = 1.0 / (1.0 + e2)
    g2 = e2 * g1
    col = lax.broadcasted_iota(jnp.int32, (x.shape[0], ROUTE_W), 1)
    route_o[...] = (jnp.where(col == 0, i1, 0.0) + jnp.where(col == 1, i2, 0.0)
                    + jnp.where(col == 2, g1, 0.0) + jnp.where(col == 3, g2, 0.0))


def _attn_out_router(o, x, p):
    N, D = x.shape
    tm = min(N, 512)
    E = p["moe_router"].shape[1]
    router = p["moe_router"].astype(F32)
    tile = pl.BlockSpec((tm, D), lambda i: (i, 0))
    return pl.pallas_call(
        _attn_out_router_kernel,
        grid=(N // tm,),
        in_specs=[tile, tile, pl.BlockSpec((D, D), lambda i: (0, 0)), pl.BlockSpec((1, D), lambda i: (0, 0)),
                  pl.BlockSpec((D, E), lambda i: (0, 0))],
        out_specs=[tile, tile, pl.BlockSpec((tm, ROUTE_W), lambda i: (i, 0))],
        out_shape=[jax.ShapeDtypeStruct((N, D), F32), jax.ShapeDtypeStruct((N, D), F32),
                   jax.ShapeDtypeStruct((N, ROUTE_W), F32)],
        compiler_params=_cparams(("parallel",)),
        name="attn_out_router",
    )(o, x, p["sb_w_o"].astype(BF16), p["norm_ffn1"].reshape(1, D).astype(F32), router)


def _moe_plan(route, n_exp, tm, n_rows):
    n = route.shape[0]
    eid = jnp.concatenate([route[:, 0], route[:, 1]]).astype(jnp.int32)
    onehot = (eid[:, None] == jnp.arange(n_exp, dtype=jnp.int32)[None, :]).astype(jnp.int32)
    cum = jnp.cumsum(onehot, axis=0)
    counts = cum[-1]
    rank = jnp.sum(onehot * (cum - 1), axis=1)
    padded = (counts + tm - 1) // tm * tm
    ends = jnp.cumsum(padded)
    dest = (ends - padded)[eid] + rank
    src = jnp.zeros((n_rows,), jnp.int32).at[dest].set(jnp.tile(jnp.arange(n, dtype=jnp.int32), TOP_K))
    tile_start = jnp.arange(n_rows // tm, dtype=jnp.int32) * tm
    tile_exp = jnp.minimum(jnp.searchsorted(ends, tile_start, side="right"), n_exp - 1).astype(jnp.int32)
    active = (tile_start < ends[-1]).astype(jnp.int32)
    return src, dest, tile_exp, active


def _row_copy(src_hbm, row, dst, slot, sem):
    return pltpu.make_async_copy(src_hbm.at[pl.ds(row, 1), :], dst.at[pl.ds(slot, 1), :], sem)


def _moe_experts_kernel(te_ref, act_ref, cur_ref, nxt_ref, x_hbm, wg_ref, wu_ref, wo_ref, o_ref,
                        xs_buf, sems, acc_scr, *, issue_step):
    i = pl.program_id(0)
    c = pl.program_id(1)
    tm = xs_buf.shape[1]
    slot = i % 2

    def gather(idx_ref, dst_slot):
        def issue(r, carry):
            _row_copy(x_hbm, idx_ref[0, r // LANES, r % LANES], xs_buf.at[dst_slot], r, sems.at[dst_slot]).start()
            return carry
        lax.fori_loop(0, tm, issue, 0, unroll=8)

    @pl.when((c == 0) & (i == 0))
    def _():
        gather(cur_ref, 0)

    @pl.when(c == 0)
    def _():
        pltpu.make_async_copy(x_hbm.at[pl.ds(0, tm), :], xs_buf.at[slot], sems.at[slot]).wait()
        acc_scr[...] = jnp.zeros_like(acc_scr)

    @pl.when((c == issue_step) & (i + 1 < pl.num_programs(0)))
    def _():
        gather(nxt_ref, 1 - slot)

    @pl.when(act_ref[i] > 0)
    def _():
        xs = xs_buf[slot].astype(BF16)
        gate = _dot(xs, wg_ref[0])
        up = _dot(xs, wu_ref[0])
        h = (gate * _sigmoid(gate) * up).astype(BF16)
        acc_scr[...] += _dot(h, wo_ref[0])

    @pl.when(c == pl.num_programs(1) - 1)
    def _():
        o_ref[...] = acc_scr[...]


def _moe_experts(xn, src, tile_exp, active, w_in, w_out, tm):
    n_rows = src.shape[0]
    D = xn.shape[1]
    F = w_out.shape[1]
    fc = _ffn_chunk(F)
    nfc = F // fc
    nt = n_rows // tm
    wi = w_in.astype(BF16)
    src3 = src.reshape(nt, tm // LANES, LANES)
    idx = (1, tm // LANES, LANES)
    grid_spec = pltpu.PrefetchScalarGridSpec(
        num_scalar_prefetch=2,
        grid=(nt, nfc),
        in_specs=[pl.BlockSpec(idx, lambda i, c, te, act: (i, 0, 0), memory_space=pltpu.SMEM),
                  pl.BlockSpec(idx, lambda i, c, te, act: (jnp.minimum(i + 1, nt - 1), 0, 0), memory_space=pltpu.SMEM),
                  pl.BlockSpec(memory_space=pl.ANY),
                  pl.BlockSpec((1, D, fc), lambda i, c, te, act: (te[i], 0, c * act[i])),
                  pl.BlockSpec((1, D, fc), lambda i, c, te, act: (te[i], 0, nfc + c * act[i])),
                  pl.BlockSpec((1, fc, D), lambda i, c, te, act: (te[i], c * act[i], 0))],
        out_specs=pl.BlockSpec((tm, D), lambda i, c, te, act: (i, 0)),
        scratch_shapes=[pltpu.VMEM((2, tm, D), F32), pltpu.SemaphoreType.DMA((2,)), pltpu.VMEM((tm, D), F32)],
    )
    return pl.pallas_call(
        functools.partial(_moe_experts_kernel, issue_step=min(1, nfc - 1)),
        grid_spec=grid_spec,
        out_shape=jax.ShapeDtypeStruct((n_rows, D), F32),
        compiler_params=_cparams(("arbitrary", "arbitrary")),
        name="moe_experts",
    )(tile_exp, active, src3, src3, xn, wi, wi, w_out.astype(BF16))


def _moe_combine_kernel(pos_ref, x_ref, route_ref, ys_hbm, o_ref, buf, sem):
    tm = x_ref.shape[0]

    def issue(t, carry):
        for k in range(TOP_K):
            _row_copy(ys_hbm, pos_ref[k, t // LANES, t % LANES], buf.at[k], t, sem).start()
        return carry

    lax.fori_loop(0, tm, issue, 0, unroll=4)
    for k in range(TOP_K):
        pltpu.make_async_copy(ys_hbm.at[pl.ds(0, tm), :], buf.at[k], sem).wait()
    route = route_ref[...]
    o_ref[...] = x_ref[...] + route[:, 2:3] * buf[0] + route[:, 3:4] * buf[1]


def _moe_combine(x, route, ys, dest, ct):
    N, D = x.shape
    pos = dest.reshape(TOP_K, N // LANES, LANES)
    return pl.pallas_call(
        _moe_combine_kernel,
        grid=(N // ct,),
        in_specs=[pl.BlockSpec((TOP_K, ct // LANES, LANES), lambda i: (0, i, 0), memory_space=pltpu.SMEM),
                  pl.BlockSpec((ct, D), lambda i: (i, 0)),
                  pl.BlockSpec((ct, ROUTE_W), lambda i: (i, 0)),
                  pl.BlockSpec(memory_space=pl.ANY)],
        out_specs=pl.BlockSpec((ct, D), lambda i: (i, 0)),
        out_shape=jax.ShapeDtypeStruct((N, D), F32),
        scratch_shapes=[pltpu.VMEM((TOP_K, ct, D), F32), pltpu.SemaphoreType.DMA(())],
        compiler_params=_cparams(("arbitrary",)),
        name="moe_combine",
    )(pos, x, route, ys)


def _moe(x, xn, route, w_in, w_out):
    N, D = x.shape
    E = w_out.shape[0]
    tm = min(N, 512)
    n_rows = TOP_K * N + E * tm
    src, dest, tile_exp, active = _moe_plan(route, E, tm, n_rows)
    ys = _moe_experts(xn, src, tile_exp, active, w_in, w_out, tm)
    return _moe_combine(x, route, ys, dest, min(N, SUBLANES * LANES))


def _decoder_group(x, shift0, wkv0, past, p):
    B, T, D = x.shape
    N = B * T
    r, lw, k, v, a, b, g, bonus, last = _rwkv_pre(x, shift0, p)
    y, s_fin = _rwkv_scan(r, lw, k, v, a, b, wkv0)
    x1 = _rwkv_post(y.reshape(N, D), bonus.reshape(N, D), g.reshape(N, D), x.reshape(N, D), p)
    x2 = _ffn(x1, p["norm_ffn0"], p["ffn_w_in"], p["ffn_w_out"])
    k_sh, v_sh, kb, vb, qb = _kvq(x2, p)
    sh3 = lambda z: z.reshape(B, T, D)
    if past is None:
        o = _sb_prompt(sh3(qb), sh3(kb), sh3(vb), p["sb_bias"])
    else:
        o = _sb_paged(sh3(qb), sh3(k_sh), sh3(v_sh), p["sb_bias"], *past)
    x3, xn3, route = _attn_out_router(o.reshape(N, D), x2, p)
    x4 = _moe(x3, xn3, route, p["moe_w_in"], p["moe_w_out"])
    H = D // HEAD_DIM
    return (x4.reshape(B, T, D), last.reshape(1, B, D), s_fin[None], k_sh.reshape(B, T, H, HEAD_DIM),
            v_sh.reshape(B, T, H, HEAD_DIM))


def kernel(x_prompt, x_sample, cache_k, cache_v, state_wkv, state_shift, page_table, norm_mix, norm_ffn, rw_mix, rw_w_r, rw_w_k, rw_w_v, rw_w_o, rw_w0, rw_w1, rw_w2, rw_a0, rw_a1, rw_a2, rw_g1, rw_g2, rw_k_k, rw_k_a, rw_r_k, rw_ln_w, rw_ln_b, kv_norm, kv_w, kv_k_norm, sb_w_q, sb_q_norm, sb_bias, sb_w_o, ffn_w_in, ffn_w_out, moe_router, moe_w_in, moe_w_out):
    assert norm_mix.shape[0] == 2 and state_wkv.shape[0] == 1, "one RWKV layer followed by one attention layer"
    p = dict(norm_mix0=norm_mix[0], norm_mix1=norm_mix[1], norm_ffn0=norm_ffn[0], norm_ffn1=norm_ffn[1],
             rw_mix=rw_mix[0], rw_w_r=rw_w_r[0], rw_w_k=rw_w_k[0], rw_w_v=rw_w_v[0], rw_w_o=rw_w_o[0],
             rw_w0=rw_w0[0], rw_w1=rw_w1[0], rw_w2=rw_w2[0], rw_a0=rw_a0[0], rw_a1=rw_a1[0], rw_a2=rw_a2[0],
             rw_g1=rw_g1[0], rw_g2=rw_g2[0], rw_k_k=rw_k_k[0], rw_k_a=rw_k_a[0], rw_r_k=rw_r_k[0].reshape(-1),
             rw_ln_w=rw_ln_w[0], rw_ln_b=rw_ln_b[0], kv_norm=kv_norm, kv_w=kv_w, kv_k_norm=kv_k_norm,
             sb_w_q=sb_w_q[0], sb_q_norm=sb_q_norm[0], sb_bias=sb_bias[0], sb_w_o=sb_w_o[0],
             ffn_w_in=ffn_w_in[0], ffn_w_out=ffn_w_out[0], moe_router=moe_router[0], moe_w_in=moe_w_in[0],
             moe_w_out=moe_w_out[0])
    bp, _, d = x_prompt.shape
    h = d // HEAD_DIM
    y_p, shift_p, wkv_p, k_p, v_p = _decoder_group(
        x_prompt, jnp.zeros((bp, d), x_prompt.dtype), jnp.zeros((bp, h, HEAD_DIM, HEAD_DIM), x_prompt.dtype), None, p)
    y_s, shift_s, wkv_s, k_s, v_s = _decoder_group(
        x_sample, state_shift[0], state_wkv[0], (cache_k, cache_v, page_table), p)
    return (y_p, y_s, wkv_p, shift_p, k_p, v_p, wkv_s, shift_s, k_s, v_s)
```

```python
import functools
import math

import jax
import jax.numpy as jnp
from jax import lax
from jax.experimental import pallas as pl
from jax.experimental.pallas import tpu as pltpu

F32 = jnp.float32
BF16 = jnp.bfloat16

HEAD_DIM = 64
LANES = 128
SUBLANES = 8
RMS_EPS = 1e-6
RW_GN_EPS = 64e-5
TOP_K = 2
ROUTE_W = 8
LOG2E = math.log2(math.e)
WALK = 4
SCAN_CHUNK = 64
VMEM_LIMIT = 52 * 1024 * 1024


def _cparams(sem):
    return pltpu.CompilerParams(dimension_semantics=sem, vmem_limit_bytes=VMEM_LIMIT)


def _dot(a, b):
    return jnp.dot(a, b, preferred_element_type=F32)


def _dot_nt(a, b):
    return lax.dot_general(a, b, (((1,), (1,)), ((), ())), preferred_element_type=F32)


def _dot_tn(a, b):
    return lax.dot_general(a, b, (((0,), (0,)), ((), ())), preferred_element_type=F32)


def _split(x, n):
    parts = []
    rem = x
    for i in range(n):
        p = rem.astype(BF16)
        parts.append(p)
        if i + 1 < n:
            rem = rem - p.astype(F32)
    return parts


def _mm(a, b, kind="nn", na=3, nb=3):
    f = {"nn": _dot, "nt": _dot_nt, "tn": _dot_tn}[kind]
    ap = _split(a, na) if a.dtype != BF16 else [a]
    bp = _split(b, nb) if b.dtype != BF16 else [b]
    keep = max(len(ap), len(bp))
    out = None
    for i, x in enumerate(ap):
        for j, y in enumerate(bp):
            if i + j < keep:
                t = f(x, y)
                out = t if out is None else out + t
    return out


def _rms(x, gain):
    return x * lax.rsqrt(jnp.mean(x * x, axis=-1, keepdims=True) + RMS_EPS) * gain


def _softplus(z):
    return jnp.maximum(z, 0.0) + jnp.log(1.0 + jnp.exp(-jnp.abs(z)))


def _softplus2(z):
    neg_abs = lax.bitcast_convert_type(lax.bitcast_convert_type(z, jnp.uint32) | jnp.uint32(0x80000000), F32)
    return jnp.maximum(z, 0.0) + jnp.log(1.0 + jnp.exp2(neg_abs)) * LOG2E


def _sigmoid(z):
    return 1.0 / (1.0 + jnp.exp(-z))


def _head_sum(x, e, et):
    s = _mm(x, e, na=2)
    return _mm(s, et, na=2)


def _head_onehot(d):
    h = d // HEAD_DIM
    e = (jnp.arange(d)[:, None] // HEAD_DIM == jnp.arange(LANES)[None, :]).astype(BF16)
    del h
    return e, e.T


def _rwkv_pre_kernel(x_ref, xp_ref, sh_ref, nm_ref, mix_ref, wr_ref, wk_ref, wv_ref, w0_ref, w1_ref,
                     w2_ref, a0_ref, a1_ref, a2_ref, g1_ref, g2_ref, kk_ref, ka_ref, rk_ref, e_ref,
                     et_ref, r_o, lw_o, k_o, v_o, a_o, b_o, g_o, bonus_o, last_o):
    t = pl.program_id(1)
    gain = nm_ref[...]
    x = x_ref[0]
    tt = x.shape[0]
    xn = _rms(x, gain)
    prev_tile_last = _rms(xp_ref[0][SUBLANES - 1:SUBLANES, :], gain)
    first = jnp.where(t == 0, sh_ref[0], prev_tile_last)
    row = lax.broadcasted_iota(jnp.int32, xn.shape, 0)
    prev = jnp.where(row == 0, first, pltpu.roll(xn, 1, 0))
    dx = prev - xn

    def mixed(i):
        return (xn + dx * mix_ref[i:i + 1, :]).astype(BF16)

    r = _dot(mixed(0), wr_ref[...])
    lw_raw = w0_ref[...] + _dot(jnp.tanh(_dot(mixed(1), w1_ref[...])).astype(BF16), w2_ref[...])
    k = _dot(mixed(2), wk_ref[...])
    v = _dot(mixed(3), wv_ref[...])
    asig = _sigmoid(a0_ref[...] + _dot(_dot(mixed(4), a1_ref[...]).astype(BF16), a2_ref[...]))
    g = _dot(_sigmoid(_dot(mixed(5), g1_ref[...])).astype(BF16), g2_ref[...])

    log_w = -_softplus(-lw_raw) - 0.5
    lw_o[0] = -jnp.exp(log_w)
    e, et = e_ref[...], et_ref[...]
    kkr = k * kk_ref[...]
    nrm = jnp.maximum(jnp.sqrt(_head_sum(kkr * kkr, e, et)), 1e-12)
    kk = kkr * (1.0 / nrm)
    kmod = k * (1.0 + (asig - 1.0) * ka_ref[...])
    r_o[0] = r
    k_o[0] = kmod
    v_o[0] = v
    a_o[0] = -kk
    b_o[0] = kk * asig
    g_o[0] = g
    bonus_o[0] = _head_sum(r * kmod * rk_ref[...], e, et) * v
    last_o[0] = xn[tt - 1:tt, :]


def _rwkv_pre(x, shift0, p):
    B, T, D = x.shape
    tt = min(T, 256)
    nt = T // tt
    e, et = _head_onehot(D)
    row = lambda a: a.reshape(1, -1).astype(F32)
    bf = lambda a: a.astype(BF16)
    consts = [row(p["norm_mix0"]), p["rw_mix"].astype(F32), bf(p["rw_w_r"]), bf(p["rw_w_k"]), bf(p["rw_w_v"]),
              row(p["rw_w0"]), bf(p["rw_w1"]), bf(p["rw_w2"]), row(p["rw_a0"]), bf(p["rw_a1"]), bf(p["rw_a2"]),
              bf(p["rw_g1"]), bf(p["rw_g2"]), row(p["rw_k_k"]), row(p["rw_k_a"]), row(p["rw_r_k"]), e, et]
    const_specs = [pl.BlockSpec(c.shape, lambda b, t, n=c.ndim: (0,) * n) for c in consts]
    tile = pl.BlockSpec((1, tt, D), lambda b, t: (b, t, 0))
    prev8 = pl.BlockSpec((1, SUBLANES, D), lambda b, t: (b, jnp.maximum(t * (tt // SUBLANES) - 1, 0), 0))
    one = pl.BlockSpec((1, 1, D), lambda b, t: (b, 0, 0))
    big = jax.ShapeDtypeStruct((B, T, D), F32)
    outs = pl.pallas_call(
        _rwkv_pre_kernel,
        grid=(B, nt),
        in_specs=[tile, prev8, one] + const_specs,
        out_specs=[tile] * 8 + [one],
        out_shape=[big] * 8 + [jax.ShapeDtypeStruct((B, 1, D), F32)],
        compiler_params=_cparams(("parallel", "arbitrary")),
        name="rwkv_pre",
    )(x, x, shift0.reshape(B, 1, D), *consts)
    return outs


def _scan_kernel(r_ref, lw_ref, k_ref, v_ref, a_ref, b_ref, s0_ref, y_ref, sT_ref, s_scr, *, nchunk, npair):
    C = SCAN_CHUNK
    c = pl.program_id(2)

    @pl.when(c == 0)
    def _():
        s_scr[...] = s0_ref[0]

    row2 = lax.broadcasted_iota(jnp.int32, (2 * C, LANES), 0)
    lane2 = lax.broadcasted_iota(jnp.int32, (2 * C, LANES), 1)
    own = (lane2 // HEAD_DIM) == (row2 // C)
    ri = lax.broadcasted_iota(jnp.int32, (2 * C, 2 * C), 0)
    ci = lax.broadcasted_iota(jnp.int32, (2 * C, 2 * C), 1)
    strict = ci < ri
    incl = ci <= ri
    eye = (ci == ri).astype(F32)
    tri = (lax.broadcasted_iota(jnp.int32, (C, C), 1) <= lax.broadcasted_iota(jnp.int32, (C, C), 0)).astype(BF16)

    def stack(x):
        return jnp.where(own, jnp.concatenate([x, x], axis=0), 0.0).astype(BF16)

    def mm(x, y, kind="nn"):
        return _mm(x.astype(BF16), y.astype(BF16), kind)

    chains = [(p, g) for p in range(npair) for g in range(nchunk)]
    w = {}
    for ch in chains:
        p, g = ch
        sl, ln = pl.ds(g * C, C), pl.ds(p * LANES, LANES)
        r, lw, k, v, a, b = (ref[0, sl, ln] for ref in (r_ref, lw_ref, k_ref, v_ref, a_ref, b_ref))
        gc = _mm(tri, lw, nb=2)
        gt = gc[C - 1:C, :]
        eneg = jnp.exp(-gc)
        efar = jnp.exp(gt - gc)
        w[ch] = dict(rt=stack(r * jnp.exp(gc)), at=stack(a * jnp.exp(gc - lw)), bt=stack(b * eneg),
                     kt=stack(k * eneg), bw=stack(b * efar), kw=stack(k * efar), vs=stack(v), dec=jnp.exp(gt))
    for ch in chains:
        c_ = w[ch]
        G = mm(jnp.concatenate([c_["at"], c_["rt"]], axis=0), jnp.concatenate([c_["bt"], c_["kt"]], axis=0), "nt")
        c_["L"] = jnp.where(strict, G[:2 * C, :2 * C], 0.0)
        c_["Lak"] = jnp.where(strict, G[:2 * C, 2 * C:], 0.0)
        c_["RB"] = jnp.where(incl, G[2 * C:, :2 * C], 0.0).astype(BF16)
        c_["RK"] = jnp.where(incl, G[2 * C:, 2 * C:], 0.0)
    for ch in chains:
        c_ = w[ch]
        c_["Tinv"] = eye + c_["L"]
        c_["A"] = mm(c_["L"], c_["L"])
        c_["LV"] = mm(c_["Lak"], c_["vs"]).astype(BF16)
        c_["Y0"] = mm(c_["RK"], c_["vs"])
        c_["Z0"] = mm(c_["vs"], c_["kw"], "tn")
    for _ in range(int(math.log2(C)) - 2):
        for ch in chains:
            c_ = w[ch]
            both = mm(c_["A"], jnp.concatenate([c_["A"], c_["Tinv"]], axis=1))
            c_["A"] = both[:, :2 * C]
            c_["Tinv"] = c_["Tinv"] + both[:, 2 * C:]
    for ch in chains:
        c_ = w[ch]
        c_["Tinv"] = c_["Tinv"] + mm(c_["A"], c_["Tinv"])
    for ch in chains:
        c_ = w[ch]
        PQ = mm(c_["Tinv"], jnp.concatenate([c_["at"], c_["LV"]], axis=1))
        c_["P"] = PQ[:, :LANES].astype(BF16)
        c_["QT"] = PQ[:, LANES:].T
    S = [s_scr[p] for p in range(npair)]
    for g in range(nchunk):
        for p in range(npair):
            c_ = w[(p, g)]
            c_["S"] = S[p].astype(BF16)
            UT = mm(c_["S"], c_["P"], "nt") + c_["QT"]
            c_["UT"] = UT
            S[p] = S[p] * c_["dec"] + mm(UT, c_["bw"]) + c_["Z0"]
    for p in range(npair):
        s_scr[p] = S[p]
    for ch in chains:
        p, g = ch
        c_ = w[ch]
        Y = mm(c_["rt"], c_["S"], "nt") + mm(c_["RB"], c_["UT"].T) + c_["Y0"]
        y_ref[0, pl.ds(g * C, C), pl.ds(p * LANES, LANES)] = Y[:C] + Y[C:]

    @pl.when(c == pl.num_programs(2) - 1)
    def _():
        sT_ref[0] = s_scr[...]


def _rwkv_scan(r, lw, k, v, a, b, s0):
    B, T, D = r.shape
    H = D // HEAD_DIM
    NP = D // LANES
    C = SCAN_CHUNK
    Tp = -(-T // C) * C
    if Tp != T:
        pad = lambda z: jnp.pad(z, ((0, 0), (0, Tp - T), (0, 0)))
        r, lw, k, v, a, b = (pad(z) for z in (r, lw, k, v, a, b))
    nchunk = next(n for n in (8, 4, 1) if Tp % (n * C) == 0)
    npair = next(n for n in (8, 4, 2, 1) if NP % n == 0 and n * nchunk <= 16)
    tb = nchunk * C
    s0p = s0.astype(F32).reshape(B, NP, 2, HEAD_DIM, HEAD_DIM)
    z = jnp.zeros_like(s0p[:, :, 0])
    s0bd = jnp.concatenate([jnp.concatenate([s0p[:, :, 0], z], axis=-1),
                            jnp.concatenate([z, s0p[:, :, 1]], axis=-1)], axis=-2)
    tile = pl.BlockSpec((1, tb, npair * LANES), lambda bb, j, c: (bb, c, j))
    st = pl.BlockSpec((1, npair, LANES, LANES), lambda bb, j, c: (bb, j, 0, 0))
    y, sT = pl.pallas_call(
        functools.partial(_scan_kernel, nchunk=nchunk, npair=npair),
        grid=(B, NP // npair, Tp // tb),
        in_specs=[tile] * 6 + [st],
        out_specs=[tile, st],
        out_shape=[jax.ShapeDtypeStruct((B, Tp, D), F32), jax.ShapeDtypeStruct((B, NP, LANES, LANES), F32)],
        scratch_shapes=[pltpu.VMEM((npair, LANES, LANES), F32)],
        compiler_params=_cparams(("parallel", "parallel", "arbitrary")),
        name="rwkv_scan",
    )(r, lw, k, v, a, b, s0bd)
    s_fin = jnp.stack([sT[:, :, :HEAD_DIM, :HEAD_DIM], sT[:, :, HEAD_DIM:, HEAD_DIM:]], axis=2)
    return y[:, :T], s_fin.reshape(B, H, HEAD_DIM, HEAD_DIM)


def _rwkv_post_kernel(y_ref, bonus_ref, g_ref, x_ref, lnw_ref, lnb_ref, wo_ref, e_ref, et_ref, o_ref):
    e, et = e_ref[...], et_ref[...]
    y = y_ref[...]
    mu = _head_sum(y, e, et) * (1.0 / HEAD_DIM)
    d = y - mu
    var = _head_sum(d * d, e, et) * (1.0 / HEAD_DIM)
    yn = d * lax.rsqrt(var + RW_GN_EPS) * lnw_ref[...] + lnb_ref[...]
    z = ((yn + bonus_ref[...]) * g_ref[...]).astype(BF16)
    o_ref[...] = x_ref[...] + _dot(z, wo_ref[...])


def _rwkv_post(y, bonus, g, x, p):
    N, D = x.shape
    tm = min(N, 512)
    e, et = _head_onehot(D)
    consts = [p["rw_ln_w"].reshape(1, D).astype(F32), p["rw_ln_b"].reshape(1, D).astype(F32),
              p["rw_w_o"].astype(BF16), e, et]
    tile = pl.BlockSpec((tm, D), lambda i: (i, 0))
    return pl.pallas_call(
        _rwkv_post_kernel,
        grid=(N // tm,),
        in_specs=[tile] * 4 + [pl.BlockSpec(c.shape, lambda i: (0, 0)) for c in consts],
        out_specs=tile,
        out_shape=jax.ShapeDtypeStruct((N, D), F32),
        compiler_params=_cparams(("parallel",)),
        name="rwkv_post",
    )(y, bonus, g, x, *consts)


def _ffn_kernel(x_ref, nw_ref, wg_ref, wu_ref, wo_ref, o_ref, xn_scr, acc_scr):
    c = pl.program_id(1)

    @pl.when(c == 0)
    def _():
        xn_scr[...] = _rms(x_ref[...], nw_ref[...]).astype(BF16)
        acc_scr[...] = jnp.zeros_like(acc_scr)

    xn = xn_scr[...]
    gate = _dot(xn, wg_ref[...])
    up = _dot(xn, wu_ref[...])
    h = (gate * _sigmoid(gate) * up).astype(BF16)
    acc_scr[...] += _dot(h, wo_ref[...])

    @pl.when(c == pl.num_programs(1) - 1)
    def _():
        o_ref[...] = x_ref[...] + acc_scr[...]


def _ffn_chunk(f):
    for fc in (1408, 896, 512, 256, 128):
        if f % fc == 0:
            return fc
    return f


def _ffn(x, norm_w, w_in, w_out):
    N, D = x.shape
    F = w_out.shape[0]
    tm = min(N, 512)
    fc = _ffn_chunk(F)
    nfc = F // fc
    return pl.pallas_call(
        _ffn_kernel,
        grid=(N // tm, nfc),
        in_specs=[pl.BlockSpec((tm, D), lambda i, c: (i, 0)),
                  pl.BlockSpec((1, D), lambda i, c: (0, 0)),
                  pl.BlockSpec((D, fc), lambda i, c: (0, c)),
                  pl.BlockSpec((D, fc), lambda i, c: (0, nfc + c)),
                  pl.BlockSpec((fc, D), lambda i, c: (c, 0))],
        out_specs=pl.BlockSpec((tm, D), lambda i, c: (i, 0)),
        out_shape=jax.ShapeDtypeStruct((N, D), F32),
        scratch_shapes=[pltpu.VMEM((tm, D), BF16), pltpu.VMEM((tm, D), F32)],
        compiler_params=_cparams(("parallel", "arbitrary")),
        name="ffn",
    )(x, norm_w.reshape(1, D).astype(F32), w_in.astype(BF16), w_in.astype(BF16), w_out.astype(BF16))


def _kvq_kernel(x_ref, kvn_ref, qn_ref, wkv_ref, wq_ref, kg_ref, qg_ref, e_ref, et_ref,
                k_o, v_o, kb_o, vb_o, qb_o):
    e, et = e_ref[...], et_ref[...]
    x = x_ref[...]
    D = x.shape[1]
    xhat = x * lax.rsqrt(jnp.mean(x * x, axis=-1, keepdims=True) + RMS_EPS)
    kv = _dot((xhat * kvn_ref[...]).astype(BF16), wkv_ref[...])
    kraw = kv[:, :D]
    v = kv[:, D:]
    k = kraw * lax.rsqrt(_head_sum(kraw * kraw, e, et) * (1.0 / HEAD_DIM) + RMS_EPS) * kg_ref[...]
    q = _dot((xhat * qn_ref[...]).astype(BF16), wq_ref[...])
    q = q * lax.rsqrt(_head_sum(q * q, e, et) * (1.0 / HEAD_DIM) + RMS_EPS) * qg_ref[...]
    k_o[...] = k
    v_o[...] = v
    kb_o[...] = k.astype(BF16)
    vb_o[...] = v.astype(BF16)
    qb_o[...] = (q * (LOG2E / math.sqrt(HEAD_DIM))).astype(BF16)


def _kvq(x, p):
    N, D = x.shape
    H = D // HEAD_DIM
    tm = min(N, 512)
    e, et = _head_onehot(D)
    row = lambda a: a.reshape(1, D).astype(F32)
    consts = [row(p["kv_norm"]), row(p["norm_mix1"]), p["kv_w"].astype(BF16), p["sb_w_q"].astype(BF16),
              row(jnp.tile(p["kv_k_norm"], H)), row(jnp.tile(p["sb_q_norm"], H)), e, et]
    tile = pl.BlockSpec((tm, D), lambda i: (i, 0))
    f = jax.ShapeDtypeStruct((N, D), F32)
    h = jax.ShapeDtypeStruct((N, D), BF16)
    return pl.pallas_call(
        _kvq_kernel,
        grid=(N // tm,),
        in_specs=[tile] + [pl.BlockSpec(c.shape, lambda i: (0, 0)) for c in consts],
        out_specs=[tile] * 5,
        out_shape=[f, f, h, h, h],
        compiler_params=_cparams(("parallel",)),
        name="kvq_proj",
    )(x, *consts)


def _sb_prompt_kernel(bias_ref, q_ref, k_ref, v_ref, o_ref, acc_scr, car_scr, *, tq):
    j = pl.program_id(1)
    i = pl.program_id(2)
    q = q_ref[0]
    lane = lax.broadcasted_iota(jnp.int32, q.shape, 1)
    qh = [jnp.where(lane < HEAD_DIM, q, jnp.zeros_like(q)), jnp.where(lane >= HEAD_DIM, q, jnp.zeros_like(q))]
    ri = lax.broadcasted_iota(jnp.int32, (tq, tq), 0)
    ci = lax.broadcasted_iota(jnp.int32, (tq, tq), 1)
    later = (ri > ci).astype(BF16)
    causal = ci < ri
    rep = tq // LANES

    def kv(kb):
        start = pl.multiple_of(kb * tq, tq)
        return k_ref[0, pl.ds(start, tq), :], v_ref[0, pl.ds(start, tq), :]

    def logits(h, kblk):
        return _dot_nt(qh[h], kblk) + bias_ref[2 * j + h]

    def sums(z, masked):
        sp = _softplus2(z)
        if masked:
            sp = jnp.where(causal, sp, 0.0)
        tail = _dot(sp.astype(BF16), later)
        total = jnp.broadcast_to(tail[:, 0:1] + sp[:, 0:1], (tq, LANES))
        return z - sp - tail, total

    def weights(part, carry, vblk, masked):
        pa = jnp.exp2(part - jnp.tile(carry, (1, rep)))
        if masked:
            pa = jnp.where(causal, pa, 0.0)
        return _dot(pa.astype(BF16), vblk)

    def walk(first, n, diagonal):
        blocks = [kv(first - t) for t in range(n)]
        masks = [diagonal and t == 0 for t in range(n)]
        zs = [[logits(h, kb) for kb, _ in blocks] for h in range(2)]
        ps = [[sums(z, m) for z, m in zip(zs[h], masks)] for h in range(2)]
        for h in range(2):
            carry = car_scr[h]
            out = None
            for (part, total), (_, vb), m in zip(ps[h], blocks, masks):
                w = weights(part, carry, vb, m)
                out = w if out is None else out + w
                carry = carry + total
            acc_scr[h] += out
            car_scr[h] = carry

    acc_scr[...] = jnp.zeros_like(acc_scr)
    car_scr[...] = jnp.zeros_like(car_scr)
    head = (i + 1) % WALK
    for n in range(1, WALK + 1):
        @pl.when(head == n % WALK)
        def _(n=n):
            walk(i, n, True)
    first_rest = i - jnp.where(head == 0, WALK, head)

    def group(s, _):
        walk(first_rest - WALK * s, WALK, False)
        return 0

    lax.fori_loop(0, (first_rest + 1) // WALK, group, 0)

    o_ref[0] = jnp.where(lane < HEAD_DIM, acc_scr[0], acc_scr[1]).astype(o_ref.dtype)


def _sb_prompt(q, k, v, bias):
    B, T, D = q.shape
    NP = D // LANES
    tq = min(T, 256)
    grid_spec = pltpu.PrefetchScalarGridSpec(
        num_scalar_prefetch=1,
        grid=(B, NP, T // tq),
        in_specs=[pl.BlockSpec((1, tq, LANES), lambda b, j, i, bias: (b, i, j)),
                  pl.BlockSpec((1, T, LANES), lambda b, j, i, bias: (b, 0, j)),
                  pl.BlockSpec((1, T, LANES), lambda b, j, i, bias: (b, 0, j))],
        out_specs=pl.BlockSpec((1, tq, LANES), lambda b, j, i, bias: (b, i, j)),
        scratch_shapes=[pltpu.VMEM((2, tq, LANES), F32), pltpu.VMEM((2, tq, LANES), F32)],
    )
    return pl.pallas_call(
        functools.partial(_sb_prompt_kernel, tq=tq),
        grid_spec=grid_spec,
        out_shape=jax.ShapeDtypeStruct((B, T, D), BF16),
        compiler_params=_cparams(("parallel", "parallel", "arbitrary")),
        name="sb_attn_prompt",
    )(bias.astype(F32) * LOG2E, q, k, v)


def _sb_paged_kernel(pt_ref, q_ref, kn_ref, vn_ref, bias_ref, later_ref, *rest, tnew, page, heads, pps):
    k_refs, v_refs = rest[0:2 * pps:2], rest[1:2 * pps:2]
    o_ref, acc_scr, car_scr = rest[2 * pps:]
    s = pl.program_id(1)

    def segment(ks, vs, mask):
        nk = len(ks) * page
        q = q_ref[0]
        slab = 2 * tnew
        tiles = lambda refs, h: jnp.concatenate([r[0, h] for r in refs], axis=1).astype(BF16)
        zs = []
        for h in range(heads):
            qq = q[(h // 2) * slab:(h // 2 + 1) * slab, :]
            zz = _dot(qq, tiles(ks, h))
            zs.append(zz[(h % 2) * tnew:(h % 2 + 1) * tnew, :])
        z = jnp.concatenate(zs, axis=0) + bias_ref[:, :nk]
        sp = _softplus2(z)
        if mask is not None:
            sp = jnp.where(mask, sp, 0.0)
        tail = _dot(sp.astype(BF16), later_ref[:nk, :nk])
        carry = car_scr[...]
        pa = jnp.exp2(z - sp - tail - jnp.tile(carry, (1, nk // LANES)))
        if mask is not None:
            pa = jnp.where(mask, pa, 0.0)
        pa = pa.astype(BF16)
        for h in range(heads):
            res = _dot_nt(pa[(h // 2) * slab:(h // 2 + 1) * slab, :], tiles(vs, h))
            acc_scr[h * tnew:(h + 1) * tnew, :] += res[(h % 2) * tnew:(h % 2 + 1) * tnew, :]
        car_scr[...] = carry + jnp.broadcast_to(tail[:, 0:1] + sp[:, 0:1], carry.shape)

    @pl.when(s == 0)
    def _():
        acc_scr[...] = jnp.zeros_like(acc_scr)
        car_scr[...] = jnp.zeros_like(car_scr)
        key = lax.broadcasted_iota(jnp.int32, (LANES, page), 1)
        qi = lax.broadcasted_iota(jnp.int32, (LANES, page), 0) % tnew
        segment([kn_ref], [vn_ref], key < qi)

    @pl.when(s > 0)
    def _():
        segment(k_refs, v_refs, None)

    @pl.when(s == pl.num_programs(1) - 1)
    def _():
        o_ref[0] = acc_scr[...]


def _sb_paged(q, k_new, v_new, bias, cache_k, cache_v, page_table):
    B, tnew, D = q.shape
    H = D // HEAD_DIM
    n_pool, page = cache_k.shape[0], cache_k.shape[1]
    npg = page_table.shape[1]
    assert H * tnew == LANES and page == LANES and tnew % SUBLANES == 0
    ck = cache_k.transpose(0, 2, 3, 1)
    cv = cache_v.transpose(0, 2, 3, 1)
    new = lambda z: jnp.pad(z.reshape(B, tnew, H, HEAD_DIM).transpose(0, 2, 3, 1),
                            ((0, 0), (0, 0), (0, 0), (0, page - tnew)))
    qrows = q.reshape(B, tnew, H, HEAD_DIM).transpose(0, 2, 1, 3).reshape(B, LANES, HEAD_DIM)
    pps = next(n for n in (8, 4, 2, 1) if npg % n == 0)
    nk = pps * page
    bias_rows = jnp.broadcast_to(jnp.repeat(bias.astype(F32) * LOG2E, tnew)[:, None], (LANES, nk))
    later = (jnp.arange(nk)[:, None] > jnp.arange(nk)[None, :]).astype(BF16)

    def page_map(slot):
        return lambda b, s, pt: (pt[b * npg + npg - jnp.maximum(s, 1) * pps + slot], 0, 0, 0)

    seq3 = lambda b, s, pt: (b, 0, 0)
    seq4 = lambda b, s, pt: (b, 0, 0, 0)
    const = lambda b, s, pt: (0, 0)
    pg = (1, H, HEAD_DIM, page)
    page_specs, page_args = [], []
    for slot in range(pps):
        page_specs += [pl.BlockSpec(pg, page_map(slot)), pl.BlockSpec(pg, page_map(slot))]
        page_args += [ck, cv]
    grid_spec = pltpu.PrefetchScalarGridSpec(
        num_scalar_prefetch=1,
        grid=(B, npg // pps + 1),
        in_specs=[pl.BlockSpec((1, LANES, HEAD_DIM), seq3),
                  pl.BlockSpec(pg, seq4), pl.BlockSpec(pg, seq4),
                  pl.BlockSpec((LANES, nk), const), pl.BlockSpec((nk, nk), const)] + page_specs,
        out_specs=pl.BlockSpec((1, LANES, HEAD_DIM), seq3),
        scratch_shapes=[pltpu.VMEM((LANES, HEAD_DIM), F32), pltpu.VMEM((LANES, LANES), F32)],
    )
    o = pl.pallas_call(
        functools.partial(_sb_paged_kernel, tnew=tnew, page=page, heads=H, pps=pps),
        grid_spec=grid_spec,
        out_shape=jax.ShapeDtypeStruct((B, LANES, HEAD_DIM), F32),
        compiler_params=_cparams(("parallel", "arbitrary")),
        name="sb_attn_paged",
    )(page_table.reshape(-1).astype(jnp.int32), qrows, new(k_new), new(v_new), bias_rows, later, *page_args)
    return o.reshape(B, H, tnew, HEAD_DIM).transpose(0, 2, 1, 3).reshape(B, tnew, D).astype(BF16)


def _attn_out_router_kernel(o_ref, x_ref, wo_ref, nw_ref, rt_ref, x_o, xn_o, route_o):
    x = x_ref[...] + _dot(o_ref[...], wo_ref[...])
    xn = _rms(x, nw_ref[...])
    x_o[...] = x
    xn_o[...] = xn
    n_exp = rt_ref.shape[1]
    logits = _mm(xn, rt_ref[...])
    lane = lax.broadcasted_iota(jnp.int32, logits.shape, 1).astype(F32)
    v1 = jnp.max(logits, axis=-1, keepdims=True)
    i1 = jnp.min(jnp.where(logits == v1, lane, float(n_exp)), axis=-1, keepdims=True)
    rest = jnp.where(lane == i1, -jnp.inf, logits)
    v2 = jnp.max(rest, axis=-1, keepdims=True)
    i2 = jnp.min(jnp.where(rest == v2, lane, float(n_exp)), axis=-1, keepdims=True)
    e2 = jnp.exp(v2 - v1)
    g1 = 1.0 / (1.0 + e2)
    g2 = e2 * g1
    col = lax.broadcasted_iota(jnp.int32, (x.shape[0], ROUTE_W), 1)
    route_o[...] = (jnp.where(col == 0, i1, 0.0) + jnp.where(col == 1, i2, 0.0)
                    + jnp.where(col == 2, g1, 0.0) + jnp.where(col == 3, g2, 0.0))


def _attn_out_router(o, x, p):
    N, D = x.shape
    tm = min(N, 512)
    E = p["moe_router"].shape[1]
    router = p["moe_router"].astype(F32)
    tile = pl.BlockSpec((tm, D), lambda i: (i, 0))
    return pl.pallas_call(
        _attn_out_router_kernel,
        grid=(N // tm,),
        in_specs=[tile, tile, pl.BlockSpec((D, D), lambda i: (0, 0)), pl.BlockSpec((1, D), lambda i: (0, 0)),
                  pl.BlockSpec((D, E), lambda i: (0, 0))],
        out_specs=[tile, tile, pl.BlockSpec((tm, ROUTE_W), lambda i: (i, 0))],
        out_shape=[jax.ShapeDtypeStruct((N, D), F32), jax.ShapeDtypeStruct((N, D), F32),
                   jax.ShapeDtypeStruct((N, ROUTE_W), F32)],
        compiler_params=_cparams(("parallel",)),
        name="attn_out_router",
    )(o, x, p["sb_w_o"].astype(BF16), p["norm_ffn1"].reshape(1, D).astype(F32), router)


def _moe_plan(route, n_exp, tm, n_rows):
    n = route.shape[0]
    eid = jnp.concatenate([route[:, 0], route[:, 1]]).astype(jnp.int32)
    onehot = (eid[:, None] == jnp.arange(n_exp, dtype=jnp.int32)[None, :]).astype(jnp.int32)
    cum = jnp.cumsum(onehot, axis=0)
    counts = cum[-1]
    rank = jnp.sum(onehot * (cum - 1), axis=1)
    padded = (counts + tm - 1) // tm * tm
    ends = jnp.cumsum(padded)
    dest = (ends - padded)[eid] + rank
    src = jnp.zeros((n_rows,), jnp.int32).at[dest].set(jnp.tile(jnp.arange(n, dtype=jnp.int32), TOP_K))
    tile_start = jnp.arange(n_rows // tm, dtype=jnp.int32) * tm
    tile_exp = jnp.minimum(jnp.searchsorted(ends, tile_start, side="right"), n_exp - 1).astype(jnp.int32)
    active = (tile_start < ends[-1]).astype(jnp.int32)
    return src, dest, tile_exp, active


def _row_copy(src_hbm, row, dst, slot, sem):
    return pltpu.make_async_copy(src_hbm.at[pl.ds(row, 1), :], dst.at[pl.ds(slot, 1), :], sem)


def _moe_experts_kernel(te_ref, act_ref, cur_ref, nxt_ref, x_hbm, wg_ref, wu_ref, wo_ref, o_ref,
                        xs_buf, sems, acc_scr, *, issue_step):
    i = pl.program_id(0)
    c = pl.program_id(1)
    tm = xs_buf.shape[1]
    slot = i % 2

    def gather(idx_ref, dst_slot):
        def issue(r, carry):
            _row_copy(x_hbm, idx_ref[0, r // LANES, r % LANES], xs_buf.at[dst_slot], r, sems.at[dst_slot]).start()
            return carry
        lax.fori_loop(0, tm, issue, 0, unroll=8)

    @pl.when((c == 0) & (i == 0))
    def _():
        gather(cur_ref, 0)

    @pl.when(c == 0)
    def _():
        pltpu.make_async_copy(x_hbm.at[pl.ds(0, tm), :], xs_buf.at[slot], sems.at[slot]).wait()
        acc_scr[...] = jnp.zeros_like(acc_scr)

    @pl.when((c == issue_step) & (i + 1 < pl.num_programs(0)))
    def _():
        gather(nxt_ref, 1 - slot)

    @pl.when(act_ref[i] > 0)
    def _():
        xs = xs_buf[slot].astype(BF16)
        gate = _dot(xs, wg_ref[0])
        up = _dot(xs, wu_ref[0])
        h = (gate * _sigmoid(gate) * up).astype(BF16)
        acc_scr[...] += _dot(h, wo_ref[0])

    @pl.when(c == pl.num_programs(1) - 1)
    def _():
        o_ref[...] = acc_scr[...]


def _moe_experts(xn, src, tile_exp, active, w_in, w_out, tm):
    n_rows = src.shape[0]
    D = xn.shape[1]
    F = w_out.shape[1]
    fc = _ffn_chunk(F)
    nfc = F // fc
    nt = n_rows // tm
    wi = w_in.astype(BF16)
    src3 = src.reshape(nt, tm // LANES, LANES)
    idx = (1, tm // LANES, LANES)
    grid_spec = pltpu.PrefetchScalarGridSpec(
        num_scalar_prefetch=2,
        grid=(nt, nfc),
        in_specs=[pl.BlockSpec(idx, lambda i, c, te, act: (i, 0, 0), memory_space=pltpu.SMEM),
                  pl.BlockSpec(idx, lambda i, c, te, act: (jnp.minimum(i + 1, nt - 1), 0, 0), memory_space=pltpu.SMEM),
                  pl.BlockSpec(memory_space=pl.ANY),
                  pl.BlockSpec((1, D, fc), lambda i, c, te, act: (te[i], 0, c * act[i])),
                  pl.BlockSpec((1, D, fc), lambda i, c, te, act: (te[i], 0, nfc + c * act[i])),
                  pl.BlockSpec((1, fc, D), lambda i, c, te, act: (te[i], c * act[i], 0))],
        out_specs=pl.BlockSpec((tm, D), lambda i, c, te, act: (i, 0)),
        scratch_shapes=[pltpu.VMEM((2, tm, D), F32), pltpu.SemaphoreType.DMA((2,)), pltpu.VMEM((tm, D), F32)],
    )
    return pl.pallas_call(
        functools.partial(_moe_experts_kernel, issue_step=min(1, nfc - 1)),
        grid_spec=grid_spec,
        out_shape=jax.ShapeDtypeStruct((n_rows, D), F32),
        compiler_params=_cparams(("arbitrary", "arbitrary")),
        name="moe_experts",
    )(tile_exp, active, src3, src3, xn, wi, wi, w_out.astype(BF16))


def _moe_combine_kernel(pos_ref, x_ref, route_ref, ys_hbm, o_ref, buf, sem):
    tm = x_ref.shape[0]

    def issue(t, carry):
        for k in range(TOP_K):
            _row_copy(ys_hbm, pos_ref[k, t // LANES, t % LANES], buf.at[k], t, sem).start()
        return carry

    lax.fori_loop(0, tm, issue, 0, unroll=4)
    for k in range(TOP_K):
        pltpu.make_async_copy(ys_hbm.at[pl.ds(0, tm), :], buf.at[k], sem).wait()
    route = route_ref[...]
    o_ref[...] = x_ref[...] + route[:, 2:3] * buf[0] + route[:, 3:4] * buf[1]


def _moe_combine(x, route, ys, dest, ct):
    N, D = x.shape
    pos = dest.reshape(TOP_K, N // LANES, LANES)
    return pl.pallas_call(
        _moe_combine_kernel,
        grid=(N // ct,),
        in_specs=[pl.BlockSpec((TOP_K, ct // LANES, LANES), lambda i: (0, i, 0), memory_space=pltpu.SMEM),
                  pl.BlockSpec((ct, D), lambda i: (i, 0)),
                  pl.BlockSpec((ct, ROUTE_W), lambda i: (i, 0)),
                  pl.BlockSpec(memory_space=pl.ANY)],
        out_specs=pl.BlockSpec((ct, D), lambda i: (i, 0)),
        out_shape=jax.ShapeDtypeStruct((N, D), F32),
        scratch_shapes=[pltpu.VMEM((TOP_K, ct, D), F32), pltpu.SemaphoreType.DMA(())],
        compiler_params=_cparams(("arbitrary",)),
        name="moe_combine",
    )(pos, x, route, ys)


def _moe(x, xn, route, w_in, w_out):
    N, D = x.shape
    E = w_out.shape[0]
    tm = min(N, 512)
    n_rows = TOP_K * N + E * tm
    src, dest, tile_exp, active = _moe_plan(route, E, tm, n_rows)
    ys = _moe_experts(xn, src, tile_exp, active, w_in, w_out, tm)
    return _moe_combine(x, route, ys, dest, min(N, SUBLANES * LANES))


def _decoder_group(x, shift0, wkv0, past, p):
    B, T, D = x.shape
    N = B * T
    r, lw, k, v, a, b, g, bonus, last = _rwkv_pre(x, shift0, p)
    y, s_fin = _rwkv_scan(r, lw, k, v, a, b, wkv0)
    x1 = _rwkv_post(y.reshape(N, D), bonus.reshape(N, D), g.reshape(N, D), x.reshape(N, D), p)
    x2 = _ffn(x1, p["norm_ffn0"], p["ffn_w_in"], p["ffn_w_out"])
    k_sh, v_sh, kb, vb, qb = _kvq(x2, p)
    sh3 = lambda z: z.reshape(B, T, D)
    if past is None:
        o = _sb_prompt(sh3(qb), sh3(kb), sh3(vb), p["sb_bias"])
    else:
        o = _sb_paged(sh3(qb), sh3(k_sh), sh3(v_sh), p["sb_bias"], *past)
    x3, xn3, route = _attn_out_router(o.reshape(N, D), x2, p)
    x4 = _moe(x3, xn3, route, p["moe_w_in"], p["moe_w_out"])
    H = D // HEAD_DIM
    return (x4.reshape(B, T, D), last.reshape(1, B, D), s_fin[None], k_sh.reshape(B, T, H, HEAD_DIM),
            v_sh.reshape(B, T, H, HEAD_DIM))


def kernel(x_prompt, x_sample, cache_k, cache_v, state_wkv, state_shift, page_table, norm_mix, norm_ffn, rw_mix, rw_w_r, rw_w_k, rw_w_v, rw_w_o, rw_w0, rw_w1, rw_w2, rw_a0, rw_a1, rw_a2, rw_g1, rw_g2, rw_k_k, rw_k_a, rw_r_k, rw_ln_w, rw_ln_b, kv_norm, kv_w, kv_k_norm, sb_w_q, sb_q_norm, sb_bias, sb_w_o, ffn_w_in, ffn_w_out, moe_router, moe_w_in, moe_w_out):
    assert norm_mix.shape[0] == 2 and state_wkv.shape[0] == 1, "one RWKV layer followed by one attention layer"
    p = dict(norm_mix0=norm_mix[0], norm_mix1=norm_mix[1], norm_ffn0=norm_ffn[0], norm_ffn1=norm_ffn[1],
             rw_mix=rw_mix[0], rw_w_r=rw_w_r[0], rw_w_k=rw_w_k[0], rw_w_v=rw_w_v[0], rw_w_o=rw_w_o[0],
             rw_w0=rw_w0[0], rw_w1=rw_w1[0], rw_w2=rw_w2[0], rw_a0=rw_a0[0], rw_a1=rw_a1[0], rw_a2=rw_a2[0],
             rw_g1=rw_g1[0], rw_g2=rw_g2[0], rw_k_k=rw_k_k[0], rw_k_a=rw_k_a[0], rw_r_k=rw_r_k[0].reshape(-1),
             rw_ln_w=rw_ln_w[0], rw_ln_b=rw_ln_b[0], kv_norm=kv_norm, kv_w=kv_w, kv_k_norm=kv_k_norm,
             sb_w_q=sb_w_q[0], sb_q_norm=sb_q_norm[0], sb_bias=sb_bias[0], sb_w_o=sb_w_o[0],
             ffn_w_in=ffn_w_in[0], ffn_w_out=ffn_w_out[0], moe_router=moe_router[0], moe_w_in=moe_w_in[0],
             moe_w_out=moe_w_out[0])
    bp, _, d = x_prompt.shape
    h = d // HEAD_DIM
    y_p, shift_p, wkv_p, k_p, v_p = _decoder_group(
        x_prompt, jnp.zeros((bp, d), x_prompt.dtype), jnp.zeros((bp, h, HEAD_DIM, HEAD_DIM), x_prompt.dtype), None, p)
    y_s, shift_s, wkv_s, k_s, v_s = _decoder_group(
        x_sample, state_shift[0], state_wkv[0], (cache_k, cache_v, page_table), p)
    return (y_p, y_s, wkv_p, shift_p, k_p, v_p, wkv_s, shift_s, k_s, v_s)
```
